```python
import math
import jax, jax.numpy as jnp
from jax import lax
import numpy as np

D_MODEL = 1024
BATCH = 16
SEQ = 2048
DEPTH = 2

N_A_LAYERS = DEPTH // 2
N_B_LAYERS = DEPTH - N_A_LAYERS
CONV_WIDTH = D_MODEL
CONV_KERNEL = 31
N_HEADS = D_MODEL // 128
HEAD_DIM = 64
V_DIM = 2 * HEAD_DIM
ATTN_WIDTH = N_HEADS * V_DIM
Q_BLOCK = 128
EPS = 1e-6

kernel_name = "yoco_conformer_diffattn_hybrid"


def _rms_norm(x, g):
    xf = x.astype(jnp.float32)
    y = xf * lax.rsqrt(jnp.mean(xf * xf, axis=-1, keepdims=True) + EPS)
    return (y * g.astype(jnp.float32)).astype(x.dtype)


def _layer_norm(x, g, b):
    xf = x.astype(jnp.float32)
    mu = jnp.mean(xf, axis=-1, keepdims=True)
    var = jnp.mean(jnp.square(xf - mu), axis=-1, keepdims=True)
    y = (xf - mu) * lax.rsqrt(var + EPS)
    return (y * g.astype(jnp.float32) + b.astype(jnp.float32)).astype(x.dtype)


def _alibi_slopes():
    i = jnp.arange(1, N_HEADS + 1, dtype=jnp.float32)
    return jnp.exp2(-8.0 * i / N_HEADS)


def _causal_depthwise_conv(u, w, b):
    k = w[:, None, :].astype(u.dtype)
    y = lax.conv_general_dilated(u, k, window_strides=(1,), padding=[(CONV_KERNEL - 1, 0)],
                                 dimension_numbers=("NWC", "WIO", "NWC"),
                                 feature_group_count=u.shape[-1])
    return y + b.astype(u.dtype)


def _conformer_conv_layer(x, g_pre, w_in, w_dw, b_dw, ln_g, ln_b, w_out, g_post):
    h = _rms_norm(x, g_pre)
    u = h @ w_in
    a, b, z = jnp.split(u, 3, axis=-1)
    c = a * jax.nn.sigmoid(b)
    c = _causal_depthwise_conv(c, w_dw, b_dw)
    c = jax.nn.silu(_layer_norm(c, ln_g, ln_b))
    y = (c * jax.nn.silu(z)) @ w_out
    return x + _rms_norm(y, g_post)


def _diff_attention(q, k, v, lam):
    bsz, seq = q.shape[0], q.shape[1]
    nblk = seq // Q_BLOCK
    qb = q.reshape(bsz, nblk, Q_BLOCK, N_HEADS, 2, HEAD_DIM).transpose(1, 0, 2, 3, 4, 5)
    slopes = _alibi_slopes()
    kpos = jnp.arange(seq)
    scale = HEAD_DIM ** -0.5

    def one_block(args):
        qi, blk = args
        qpos = blk * Q_BLOCK + jnp.arange(Q_BLOCK)
        dist = (qpos[:, None] - kpos[None, :]).astype(jnp.float32)
        bias = jnp.where(dist[None] >= 0, -slopes[:, None, None] * dist[None], -jnp.inf)
        s = jnp.einsum("bqhcd,bkhcd->bhcqk", qi, k, preferred_element_type=jnp.float32)
        p = jax.nn.softmax(s * scale + bias[None, :, None], axis=-1)
        attn = p[:, :, 0] - lam * p[:, :, 1]
        return jnp.einsum("bhqk,bkhe->bqhe", attn.astype(v.dtype), v)

    out = lax.map(one_block, (qb, jnp.arange(nblk)))
    return out.transpose(1, 0, 2, 3, 4).reshape(bsz, seq, N_HEADS, V_DIM)


def _diff_attn_layer(x, k, v, layer_idx, g_pre, w_in, lam_p, g_sub, w_out, g_post):
    bsz, seq = x.shape[0], x.shape[1]
    h = _rms_norm(x, g_pre)
    u = h @ w_in
    q = u[..., :2 * N_HEADS * HEAD_DIM].reshape(bsz, seq, N_HEADS, 2, HEAD_DIM)
    z = u[..., 2 * N_HEADS * HEAD_DIM:]
    lam_init = 0.8 - 0.6 * math.exp(-0.3 * layer_idx)
    lp = lam_p.astype(jnp.float32)
    lam = jnp.exp(jnp.sum(lp[0] * lp[1])) - jnp.exp(jnp.sum(lp[2] * lp[3])) + lam_init
    o = _diff_attention(q, k, v, lam)
    o = _rms_norm(o, g_sub) * (1.0 - lam_init)
    o = o.reshape(bsz, seq, ATTN_WIDTH).astype(x.dtype)
    y = (o * jax.nn.silu(z)) @ w_out
    return x + _rms_norm(y, g_post)


def setup_inputs(seed: int = 0) -> dict:
    key = jax.random.key(seed)
    ks = jax.random.split(key, 20)
    D, E, K = D_MODEL, CONV_WIDTH, CONV_KERNEL
    nA, nB = N_A_LAYERS, N_B_LAYERS
    nrm = lambda k, shape, fan: jax.random.normal(k, shape, jnp.float32) * fan ** -0.5
    gain = lambda k, shape: 1.0 + 0.02 * jax.random.normal(k, shape, jnp.float32)
    kv_cols = 2 * N_HEADS * HEAD_DIM + ATTN_WIDTH
    q_cols = 2 * N_HEADS * HEAD_DIM + ATTN_WIDTH
    return {
        "x": jax.random.normal(ks[0], (BATCH, SEQ, D), jnp.float32),
        "a_g_pre": gain(ks[1], (nA, D)),
        "a_w_in": nrm(ks[2], (nA, D, 3 * E), D),
        "a_w_dw": nrm(ks[3], (nA, K, E), K),
        "a_b_dw": 0.02 * jax.random.normal(ks[4], (nA, E), jnp.float32),
        "a_ln_g": gain(ks[5], (nA, E)),
        "a_ln_b": 0.02 * jax.random.normal(ks[6], (nA, E), jnp.float32),
        "a_w_out": nrm(ks[7], (nA, E, D), E),
        "a_g_post": gain(ks[8], (nA, D)),
        "kv_g": gain(ks[9], (D,)),
        "w_kv": nrm(ks[10], (D, kv_cols), D),
        "b_g_pre": gain(ks[11], (nB, D)),
        "b_w_in": nrm(ks[12], (nB, D, q_cols), D),
        "b_lambda": 0.1 * jax.random.normal(ks[13], (nB, 4, HEAD_DIM), jnp.float32),
        "b_g_sub": gain(ks[14], (nB, V_DIM)),
        "b_w_out": nrm(ks[15], (nB, ATTN_WIDTH, D), ATTN_WIDTH),
        "b_g_post": gain(ks[16], (nB, D)),
    }


def reference(x, a_g_pre, a_w_in, a_w_dw, a_b_dw, a_ln_g, a_ln_b, a_w_out, a_g_post,
              kv_g, w_kv, b_g_pre, b_w_in, b_lambda, b_g_sub, b_w_out, b_g_post):
    bsz, seq = x.shape[0], x.shape[1]
    k = v = None
    for l in range(DEPTH):
        if l < N_A_LAYERS:
            x = _conformer_conv_layer(x, a_g_pre[l], a_w_in[l], a_w_dw[l], a_b_dw[l],
                                      a_ln_g[l], a_ln_b[l], a_w_out[l], a_g_post[l])
        else:
            if l == N_A_LAYERS:
                kvh = _rms_norm(x, kv_g) @ w_kv
                k = kvh[..., :2 * N_HEADS * HEAD_DIM].reshape(bsz, seq, N_HEADS, 2, HEAD_DIM)
                v = kvh[..., 2 * N_HEADS * HEAD_DIM:].reshape(bsz, seq, N_HEADS, V_DIM)
            j = l - N_A_LAYERS
            x = _diff_attn_layer(x, k, v, l + 1, b_g_pre[j], b_w_in[j], b_lambda[j],
                                 b_g_sub[j], b_w_out[j], b_g_post[j])
    return x
```

```python
import functools
import math

import jax
import jax.numpy as jnp
from jax import lax
from jax.experimental import pallas as pl
from jax.experimental.pallas import tpu as pltpu

EPS = 1e-6
HEAD_DIM = 64
V_DIM = 2 * HEAD_DIM
CONV_HALO = 32
CONV_ROWS = 16
LANES = 128
NEG_BIG = -1e30
VMEM_LIMIT = 56 * 1024 * 1024


def _rms(x, g):
    return x * lax.rsqrt(jnp.mean(x * x, axis=-1, keepdims=True) + EPS) * g


def _sigmoid(x):
    return 0.5 * jnp.tanh(0.5 * x) + 0.5


def _conv_layer_kernel(x_ref, gpre_ref, win_ref, wdw_ref, bdw_ref, lng_ref, lnb_ref,
                       wout_ref, gpost_ref, o_ref, cbuf_ref, zbuf_ref, gbuf_ref, *,
                       ts, width, taps):
    t = pl.program_id(1)
    n_slab = width // LANES
    x = x_ref[0]
    h = _rms(x, gpre_ref[...]).astype(jnp.bfloat16)
    u = jnp.dot(h, win_ref[...], preferred_element_type=jnp.float32)
    a = u[:, :width]
    b = u[:, width:2 * width]
    z = u[:, 2 * width:]

    @pl.when(t == 0)
    def _():
        cbuf_ref[:, 0:CONV_HALO, :] = jnp.zeros((n_slab, CONV_HALO, LANES), jnp.float32)

    @pl.when(t > 0)
    def _():
        cbuf_ref[:, 0:CONV_HALO, :] = cbuf_ref[:, ts:ts + CONV_HALO, :]

    c = a * _sigmoid(b)
    for j in range(n_slab):
        cbuf_ref[j, CONV_HALO:CONV_HALO + ts, :] = c[:, j * LANES:(j + 1) * LANES]
    zbuf_ref[...] = z * _sigmoid(z)

    first = CONV_HALO - (taps - 1)

    def chunk(i, carry):
        r0 = pl.multiple_of(i * CONV_ROWS, CONV_ROWS)
        accs = []
        for j in range(n_slab):
            acc = jnp.broadcast_to(bdw_ref[j], (CONV_ROWS, LANES))
            for k in range(taps):
                acc = acc + wdw_ref[j, k:k + 1, :] * cbuf_ref[j, pl.ds(r0 + (first + k), CONV_ROWS), :]
            accs.append(acc)
        inv_w = 1.0 / width
        mu = jnp.sum(sum(accs), axis=-1, keepdims=True) * inv_w
        devs = [acc - mu for acc in accs]
        var = jnp.sum(sum(d * d for d in devs), axis=-1, keepdims=True) * inv_w
        rstd = lax.rsqrt(var + EPS)
        for j in range(n_slab):
            y = devs[j] * rstd * lng_ref[j] + lnb_ref[j]
            y = y * _sigmoid(y)
            cols = slice(j * LANES, (j + 1) * LANES)
            gbuf_ref[pl.ds(r0, CONV_ROWS), cols] = (
                y * zbuf_ref[pl.ds(r0, CONV_ROWS), cols]).astype(jnp.bfloat16)
        return carry

    lax.fori_loop(0, ts // CONV_ROWS, chunk, 0)

    y = jnp.dot(gbuf_ref[...], wout_ref[...], preferred_element_type=jnp.float32)
    o_ref[0] = x + _rms(y, gpost_ref[...])


def _conv_layer(x, g_pre, w_in, w_dw, b_dw, ln_g, ln_b, w_out, g_post, *, ts=512):
    bsz, seq, d = x.shape
    taps, width = w_dw.shape
    n_slab = width // LANES
    assert taps - 1 <= CONV_HALO and seq % ts == 0 and ts % CONV_ROWS == 0
    row = lambda v: v.reshape(1, -1)
    slab = lambda v: v.reshape(n_slab, 1, LANES)
    full = lambda shape: pl.BlockSpec(shape, lambda b, t: (0,) * len(shape))
    w_dw_slabs = w_dw.reshape(taps, n_slab, LANES).transpose(1, 0, 2)
    return pl.pallas_call(
        functools.partial(_conv_layer_kernel, ts=ts, width=width, taps=taps),
        grid=(bsz, seq // ts),
        in_specs=[
            pl.BlockSpec((1, ts, d), lambda b, t: (b, t, 0)),
            full((1, d)), full((d, 3 * width)), full((n_slab, taps, LANES)),
            full((n_slab, 1, LANES)), full((n_slab, 1, LANES)), full((n_slab, 1, LANES)),
            full((width, d)), full((1, d)),
        ],
        out_specs=pl.BlockSpec((1, ts, d), lambda b, t: (b, t, 0)),
        out_shape=jax.ShapeDtypeStruct(x.shape, jnp.float32),
        scratch_shapes=[
            pltpu.VMEM((n_slab, CONV_HALO + ts, LANES), jnp.float32),
            pltpu.VMEM((ts, width), jnp.float32),
            pltpu.VMEM((ts, width), jnp.bfloat16),
        ],
        compiler_params=pltpu.CompilerParams(
            dimension_semantics=("arbitrary", "arbitrary"), vmem_limit_bytes=VMEM_LIMIT),
        name="conv_layer",
    )(x, row(g_pre), w_in.astype(jnp.bfloat16), w_dw_slabs, slab(b_dw), slab(ln_g), slab(ln_b),
      w_out.astype(jnp.bfloat16), row(g_post))


def _proj_kernel(x_ref, g_ref, w_ref, lo_ref, hi_ref, *, lo_scale):
    h = _rms(x_ref[...], g_ref[...]).astype(jnp.bfloat16)
    u = jnp.dot(h, w_ref[...], preferred_element_type=jnp.float32)
    half = lo_ref.shape[-1]
    lo_ref[...] = (u[:, :half] * lo_scale).astype(lo_ref.dtype)
    hi_ref[...] = u[:, half:].astype(hi_ref.dtype)


def _proj(x2, g, w, *, lo_scale, lo_dtype, hi_dtype, tm=512):
    n, d = x2.shape
    cols = w.shape[1]
    half = cols // 2
    return pl.pallas_call(
        functools.partial(_proj_kernel, lo_scale=lo_scale),
        grid=(n // tm,),
        in_specs=[
            pl.BlockSpec((tm, d), lambda i: (i, 0)),
            pl.BlockSpec((1, d), lambda i: (0, 0)),
            pl.BlockSpec((d, cols), lambda i: (0, 0)),
        ],
        out_specs=[pl.BlockSpec((tm, half), lambda i: (i, 0)),
                   pl.BlockSpec((tm, cols - half), lambda i: (i, 0))],
        out_shape=[jax.ShapeDtypeStruct((n, half), lo_dtype),
                   jax.ShapeDtypeStruct((n, cols - half), hi_dtype)],
        compiler_params=pltpu.CompilerParams(
            dimension_semantics=("arbitrary",), vmem_limit_bytes=VMEM_LIMIT),
        name="proj",
    )(x2, g.reshape(1, -1), w.astype(jnp.bfloat16))


def _attn_kernel(slope_ref, q_ref, k_ref, v_ref, lam_ref, gsub_ref, o_ref,
                 m_ref, l_ref, acc_ref, *, tq, lam_init):
    h = pl.program_id(1)
    qi = pl.program_id(2)
    slope = slope_ref[h]

    q = q_ref[0]
    lane = lax.broadcasted_iota(jnp.int32, q.shape, 1)
    zero = jnp.zeros_like(q)
    qc = (jnp.where(lane < HEAD_DIM, q, zero), jnp.where(lane >= HEAD_DIM, q, zero))

    row = lax.broadcasted_iota(jnp.int32, (tq, tq), 0)
    col = lax.broadcasted_iota(jnp.int32, (tq, tq), 1)
    rel_bias = (col - row).astype(jnp.float32) * slope

    m_ref[...] = jnp.full(m_ref.shape, NEG_BIG, jnp.float32)
    l_ref[...] = jnp.zeros(l_ref.shape, jnp.float32)
    acc_ref[...] = jnp.zeros(acc_ref.shape, jnp.float32)

    def step(kb, masked):
        k0 = pl.multiple_of(kb * tq, tq)
        k = k_ref[0, pl.ds(k0, tq), :]
        v = v_ref[0, pl.ds(k0, tq), :]
        off = -slope * ((qi - kb) * tq).astype(jnp.float32)
        for c in range(2):
            s = lax.dot_general(qc[c], k, (((1,), (1,)), ((), ())),
                                preferred_element_type=jnp.float32) + rel_bias
            if masked:
                s = jnp.where(col <= row, s, NEG_BIG)
            m_old = m_ref[c]
            m_new = jnp.maximum(m_old, jnp.max(s, axis=-1, keepdims=True) + off)
            alpha = jnp.exp(m_old - m_new)
            p = jnp.exp(s - (m_new - off))
            l_ref[c] = alpha * l_ref[c] + jnp.sum(p, axis=-1, keepdims=True)
            acc_ref[c] = alpha * acc_ref[c] + jnp.dot(
                p.astype(jnp.bfloat16), v, preferred_element_type=jnp.float32)
            m_ref[c] = m_new

    def body(kb, carry):
        step(kb, False)
        return carry

    lax.fori_loop(0, qi, body, 0)
    step(qi, True)

    lp = lam_ref[...]
    lam = (jnp.exp(jnp.sum(lp[0:1] * lp[1:2], axis=-1, keepdims=True))
           - jnp.exp(jnp.sum(lp[2:3] * lp[3:4], axis=-1, keepdims=True)) + lam_init)
    o = acc_ref[0] / l_ref[0] - lam * (acc_ref[1] / l_ref[1])
    o_ref[0] = _rms(o, gsub_ref[...]) * (1.0 - lam_init)


def _attention(q, k, v, lam_p, g_sub, *, n_heads, lam_init, tq=256):
    bsz, seq, _ = q.shape
    slopes = jnp.exp2(-8.0 * jnp.arange(1, n_heads + 1, dtype=jnp.float32) / n_heads)
    grid_spec = pltpu.PrefetchScalarGridSpec(
        num_scalar_prefetch=1,
        grid=(bsz, n_heads, seq // tq),
        in_specs=[
            pl.BlockSpec((1, tq, V_DIM), lambda b, h, i, s: (b, i, h)),
            pl.BlockSpec((1, seq, V_DIM), lambda b, h, i, s: (b, 0, h)),
            pl.BlockSpec((1, seq, V_DIM), lambda b, h, i, s: (b, 0, h)),
            pl.BlockSpec((4, HEAD_DIM), lambda b, h, i, s: (0, 0)),
            pl.BlockSpec((1, V_DIM), lambda b, h, i, s: (0, 0)),
        ],
        out_specs=pl.BlockSpec((1, tq, V_DIM), lambda b, h, i, s: (b, i, h)),
        scratch_shapes=[
            pltpu.VMEM((2, tq, 1), jnp.float32),
            pltpu.VMEM((2, tq, 1), jnp.float32),
            pltpu.VMEM((2, tq, V_DIM), jnp.float32),
        ],
    )
    return pl.pallas_call(
        functools.partial(_attn_kernel, tq=tq, lam_init=lam_init),
        grid_spec=grid_spec,
        out_shape=jax.ShapeDtypeStruct((bsz, seq, n_heads * V_DIM), jnp.float32),
        compiler_params=pltpu.CompilerParams(
            dimension_semantics=("arbitrary", "arbitrary", "arbitrary"),
            vmem_limit_bytes=VMEM_LIMIT),
        name="diff_attn",
    )(slopes, q, k, v, lam_p, g_sub.reshape(1, -1))


def _attn_out_kernel(x_ref, o_ref, z_ref, w_ref, g_ref, y_ref):
    z = z_ref[...]
    gated = (o_ref[...] * (z * _sigmoid(z))).astype(jnp.bfloat16)
    y = jnp.dot(gated, w_ref[...], preferred_element_type=jnp.float32)
    y_ref[...] = x_ref[...] + _rms(y, g_ref[...])


def _attn_out(x2, o2, z2, w_out, g_post, *, tm=512):
    n, d = x2.shape
    width = o2.shape[1]
    tile = lambda c: pl.BlockSpec((tm, c), lambda i: (i, 0))
    return pl.pallas_call(
        _attn_out_kernel,
        grid=(n // tm,),
        in_specs=[tile(d), tile(width), tile(width),
                  pl.BlockSpec((width, d), lambda i: (0, 0)),
                  pl.BlockSpec((1, d), lambda i: (0, 0))],
        out_specs=tile(d),
        out_shape=jax.ShapeDtypeStruct((n, d), jnp.float32),
        compiler_params=pltpu.CompilerParams(
            dimension_semantics=("arbitrary",), vmem_limit_bytes=VMEM_LIMIT),
        name="attn_out",
    )(x2, o2, z2, w_out.astype(jnp.bfloat16), g_post.reshape(1, -1))


def kernel(x, a_g_pre, a_w_in, a_w_dw, a_b_dw, a_ln_g, a_ln_b, a_w_out, a_g_post,
           kv_g, w_kv, b_g_pre, b_w_in, b_lambda, b_g_sub, b_w_out, b_g_post):
    bsz, seq, d = x.shape
    n_a, n_b = a_w_in.shape[0], b_w_in.shape[0]
    n_heads = b_w_out.shape[1] // V_DIM
    assert w_kv.shape[1] == 2 * n_heads * V_DIM and b_w_in.shape[2] == 2 * n_heads * V_DIM

    for l in range(n_a):
        x = _conv_layer(x, a_g_pre[l], a_w_in[l], a_w_dw[l], a_b_dw[l], a_ln_g[l], a_ln_b[l],
                        a_w_out[l], a_g_post[l])

    x2 = x.reshape(bsz * seq, d)
    k = v = None
    for j in range(n_b):
        if j == 0:
            k, v = _proj(x2, kv_g, w_kv, lo_scale=1.0,
                         lo_dtype=jnp.bfloat16, hi_dtype=jnp.bfloat16)
            k = k.reshape(bsz, seq, -1)
            v = v.reshape(bsz, seq, -1)
        q, z = _proj(x2, b_g_pre[j], b_w_in[j], lo_scale=HEAD_DIM ** -0.5,
                     lo_dtype=jnp.bfloat16, hi_dtype=jnp.float32)
        lam_init = 0.8 - 0.6 * math.exp(-0.3 * (n_a + j + 1))
        o = _attention(q.reshape(bsz, seq, -1), k, v, b_lambda[j], b_g_sub[j],
                       n_heads=n_heads, lam_init=lam_init)
        x2 = _attn_out(x2, o.reshape(bsz * seq, -1), z, b_w_out[j], b_g_post[j])
    return x2.reshape(bsz, seq, d)
```

```python
import functools
import math

import jax
import jax.numpy as jnp
from jax import lax
from jax.experimental import pallas as pl
from jax.experimental.pallas import tpu as pltpu

EPS = 1e-6
HEAD_DIM = 64
V_DIM = 2 * HEAD_DIM
CONV_HALO = 32
CONV_ROWS = 16
LANES = 128
POS_SPLIT = 256
ONES_ROWS = 16
NEG_BIG = -1e30
VMEM_LIMIT = 56 * 1024 * 1024


def _rms(x, g):
    return x * lax.rsqrt(jnp.mean(x * x, axis=-1, keepdims=True) + EPS) * g


def _sigmoid(x):
    return 0.5 * jnp.tanh(0.5 * x) + 0.5


def _conv_layer_kernel(x_ref, gpre_ref, win_ref, wdw_ref, bdw_ref, lng_ref, lnb_ref,
                       wout_ref, gpost_ref, o_ref, cbuf_ref, zbuf_ref, gbuf_ref, *,
                       ts, width, taps):
    t = pl.program_id(1)
    n_slab = width // LANES
    x = x_ref[0]
    h = _rms(x, gpre_ref[...]).astype(jnp.bfloat16)
    u = jnp.dot(h, win_ref[...], preferred_element_type=jnp.float32)
    a = u[:, :width]
    b = u[:, width:2 * width]
    z = u[:, 2 * width:]

    @pl.when(t == 0)
    def _():
        cbuf_ref[:, 0:CONV_HALO, :] = jnp.zeros((n_slab, CONV_HALO, LANES), jnp.float32)

    @pl.when(t > 0)
    def _():
        cbuf_ref[:, 0:CONV_HALO, :] = cbuf_ref[:, ts:ts + CONV_HALO, :]

    c = a * _sigmoid(b)
    for j in range(n_slab):
        cbuf_ref[j, CONV_HALO:CONV_HALO + ts, :] = c[:, j * LANES:(j + 1) * LANES]
    zbuf_ref[...] = z * _sigmoid(z)

    first = CONV_HALO - (taps - 1)

    def chunk(i, carry):
        r0 = pl.multiple_of(i * CONV_ROWS, CONV_ROWS)
        accs = []
        for j in range(n_slab):
            acc = jnp.broadcast_to(bdw_ref[j], (CONV_ROWS, LANES))
            for k in range(taps):
                acc = acc + wdw_ref[j, k:k + 1, :] * cbuf_ref[j, pl.ds(r0 + (first + k), CONV_ROWS), :]
            accs.append(acc)
        inv_w = 1.0 / width
        mu = jnp.sum(sum(accs), axis=-1, keepdims=True) * inv_w
        devs = [acc - mu for acc in accs]
        var = jnp.sum(sum(d * d for d in devs), axis=-1, keepdims=True) * inv_w
        rstd = lax.rsqrt(var + EPS)
        for j in range(n_slab):
            y = devs[j] * rstd * lng_ref[j] + lnb_ref[j]
            y = y * _sigmoid(y)
            cols = slice(j * LANES, (j + 1) * LANES)
            gbuf_ref[pl.ds(r0, CONV_ROWS), cols] = (
                y * zbuf_ref[pl.ds(r0, CONV_ROWS), cols]).astype(jnp.bfloat16)
        return carry

    lax.fori_loop(0, ts // CONV_ROWS, chunk, 0)

    y = jnp.dot(gbuf_ref[...], wout_ref[...], preferred_element_type=jnp.float32)
    o_ref[0] = x + _rms(y, gpost_ref[...])


def _conv_layer(x, g_pre, w_in, w_dw, b_dw, ln_g, ln_b, w_out, g_post, *, ts=512):
    bsz, seq, d = x.shape
    taps, width = w_dw.shape
    n_slab = width // LANES
    assert taps - 1 <= CONV_HALO and seq % ts == 0 and ts % CONV_ROWS == 0
    row = lambda v: v.reshape(1, -1)
    slab = lambda v: v.reshape(n_slab, 1, LANES)
    full = lambda shape: pl.BlockSpec(shape, lambda b, t: (0,) * len(shape))
    w_dw_slabs = w_dw.reshape(taps, n_slab, LANES).transpose(1, 0, 2)
    return pl.pallas_call(
        functools.partial(_conv_layer_kernel, ts=ts, width=width, taps=taps),
        grid=(bsz, seq // ts),
        in_specs=[
            pl.BlockSpec((1, ts, d), lambda b, t: (b, t, 0)),
            full((1, d)), full((d, 3 * width)), full((n_slab, taps, LANES)),
            full((n_slab, 1, LANES)), full((n_slab, 1, LANES)), full((n_slab, 1, LANES)),
            full((width, d)), full((1, d)),
        ],
        out_specs=pl.BlockSpec((1, ts, d), lambda b, t: (b, t, 0)),
        out_shape=jax.ShapeDtypeStruct(x.shape, jnp.float32),
        scratch_shapes=[
            pltpu.VMEM((n_slab, CONV_HALO + ts, LANES), jnp.float32),
            pltpu.VMEM((ts, width), jnp.float32),
            pltpu.VMEM((ts, width), jnp.bfloat16),
        ],
        compiler_params=pltpu.CompilerParams(
            dimension_semantics=("arbitrary", "arbitrary"), vmem_limit_bytes=VMEM_LIMIT),
        name="conv_layer",
    )(x, row(g_pre), w_in.astype(jnp.bfloat16), w_dw_slabs, slab(b_dw), slab(ln_g), slab(ln_b),
      w_out.astype(jnp.bfloat16), row(g_post))


def _proj_kernel(x_ref, gains_ref, *refs, outs):
    w_refs, o_refs = refs[:len(outs)], refs[len(outs):]
    x = x_ref[0]
    xn = x * lax.rsqrt(jnp.mean(x * x, axis=-1, keepdims=True) + EPS)
    hs = {}
    for (g, feature_major, scale), w_ref, o_ref in zip(outs, w_refs, o_refs):
        if g not in hs:
            hs[g] = (xn * gains_ref[g:g + 1, :]).astype(jnp.bfloat16)
        if feature_major:
            u = lax.dot_general(w_ref[...], hs[g], (((1,), (1,)), ((), ())),
                                preferred_element_type=jnp.float32)
        else:
            u = jnp.dot(hs[g], w_ref[...], preferred_element_type=jnp.float32)
        if scale != 1.0:
            u = u * scale
        o_ref[0] = u.astype(o_ref.dtype)


def _proj(x, gains, weights, outs, dtypes, *, tm=512):
    bsz, seq, d = x.shape
    in_specs = [pl.BlockSpec((1, tm, d), lambda b, t: (b, t, 0)),
                pl.BlockSpec(gains.shape, lambda b, t: (0, 0))]
    out_specs, out_shapes = [], []
    for (g, feature_major, scale), w, dt in zip(outs, weights, dtypes):
        in_specs.append(pl.BlockSpec(w.shape, lambda b, t: (0, 0)))
        if feature_major:
            cols = w.shape[0]
            out_specs.append(pl.BlockSpec((1, cols, tm), lambda b, t: (b, 0, t)))
            out_shapes.append(jax.ShapeDtypeStruct((bsz, cols, seq), dt))
        else:
            cols = w.shape[1]
            out_specs.append(pl.BlockSpec((1, tm, cols), lambda b, t: (b, t, 0)))
            out_shapes.append(jax.ShapeDtypeStruct((bsz, seq, cols), dt))
    return pl.pallas_call(
        functools.partial(_proj_kernel, outs=tuple(outs)),
        grid=(bsz, seq // tm),
        in_specs=in_specs, out_specs=out_specs, out_shape=out_shapes,
        compiler_params=pltpu.CompilerParams(
            dimension_semantics=("arbitrary", "arbitrary"), vmem_limit_bytes=VMEM_LIMIT),
        name="proj",
    )(x, gains, *[w.astype(jnp.bfloat16) for w in weights])


def _attn_kernel(slope_ref, qt_ref, k_ref, vt_ref, zt_ref, lam_ref, gsub_ref, gt_ref,
                 kaug_ref, vaug_ref, *, tq, lam_init):
    first = (pl.program_id(0) == 0) & (pl.program_id(1) == 0)
    seq = k_ref.shape[1]
    slope = slope_ref[pl.program_id(1)]

    @pl.when(first)
    def _():
        j = lax.broadcasted_iota(jnp.int32, (seq, V_DIM), 0)
        lane = lax.broadcasted_iota(jnp.int32, (seq, V_DIM), 1)
        pos = jnp.where(lane == 0, j // POS_SPLIT, jnp.where(lane == 1, j % POS_SPLIT, 0))
        kaug_ref[:, V_DIM:] = pos.astype(jnp.float32).astype(jnp.bfloat16)
        vaug_ref[V_DIM:, :] = jnp.ones((ONES_ROWS, seq), jnp.bfloat16)

    kaug_ref[:, :V_DIM] = k_ref[0]
    vaug_ref[:V_DIM, :] = vt_ref[0]

    r = lax.broadcasted_iota(jnp.int32, (ONES_ROWS, tq), 0)
    pos_rows = jnp.where(r == 0, slope * POS_SPLIT, jnp.where(r == 1, slope, 0.0)).astype(jnp.bfloat16)
    zeros_half = jnp.zeros((HEAD_DIM, tq), jnp.bfloat16)
    zeros_tail = jnp.zeros((V_DIM - ONES_ROWS, tq), jnp.bfloat16)
    key_i = lax.broadcasted_iota(jnp.int32, (tq, tq), 0)
    qry_i = lax.broadcasted_iota(jnp.int32, (tq, tq), 1)
    causal = key_i <= qry_i

    lp = lam_ref[...]
    lam = (jnp.exp(jnp.sum(lp[0:1] * lp[1:2], axis=-1, keepdims=True))
           - jnp.exp(jnp.sum(lp[2:3] * lp[3:4], axis=-1, keepdims=True)) + lam_init)

    for qi in range(seq // tq):
        cols = slice(qi * tq, (qi + 1) * tq)
        nk = (qi + 1) * tq
        qt = qt_ref[0, :, cols]
        outs = []
        for c in range(2):
            halves = (qt[:HEAD_DIM], zeros_half) if c == 0 else (zeros_half, qt[HEAD_DIM:])
            qa = jnp.concatenate([*halves, pos_rows, zeros_tail], axis=0)
            s = jnp.dot(kaug_ref[0:nk, :], qa, preferred_element_type=jnp.float32)
            diag = jnp.where(causal, s[nk - tq:], NEG_BIG)
            s = diag if qi == 0 else jnp.concatenate([s[:nk - tq], diag], axis=0)
            m = jnp.max(s, axis=0, keepdims=True)
            p = jnp.exp(s - m).astype(jnp.bfloat16)
            acc = jnp.dot(vaug_ref[:, 0:nk], p, preferred_element_type=jnp.float32)
            outs.append(acc[:V_DIM] * (1.0 / acc[V_DIM:V_DIM + 1]))
        o = outs[0] - lam * outs[1]
        ms = jnp.mean(o * o, axis=0, keepdims=True)
        o = o * lax.rsqrt(ms + EPS) * gsub_ref[...] * (1.0 - lam_init)
        z = zt_ref[0, :, cols]
        gt_ref[0, :, cols] = (o * (z * _sigmoid(z))).astype(jnp.bfloat16)


def _attention(qt, k, vt, zt, lam_p, g_sub, *, n_heads, lam_init, tq=256):
    bsz, seq, _ = k.shape
    assert seq % tq == 0 and seq <= POS_SPLIT * POS_SPLIT
    slopes = jnp.exp2(-8.0 * jnp.arange(1, n_heads + 1, dtype=jnp.float32) / n_heads)
    feat = pl.BlockSpec((1, V_DIM, seq), lambda b, h, s: (b, h, 0))
    grid_spec = pltpu.PrefetchScalarGridSpec(
        num_scalar_prefetch=1,
        grid=(bsz, n_heads),
        in_specs=[
            feat,
            pl.BlockSpec((1, seq, V_DIM), lambda b, h, s: (b, 0, h)),
            feat, feat,
            pl.BlockSpec((4, HEAD_DIM), lambda b, h, s: (0, 0)),
            pl.BlockSpec((V_DIM, tq), lambda b, h, s: (0, 0)),
        ],
        out_specs=feat,
        scratch_shapes=[
            pltpu.VMEM((seq, 2 * V_DIM), jnp.bfloat16),
            pltpu.VMEM((V_DIM + ONES_ROWS, seq), jnp.bfloat16),
        ],
    )
    return pl.pallas_call(
        functools.partial(_attn_kernel, tq=tq, lam_init=lam_init),
        grid_spec=grid_spec,
        out_shape=jax.ShapeDtypeStruct((bsz, n_heads * V_DIM, seq), jnp.bfloat16),
        compiler_params=pltpu.CompilerParams(
            dimension_semantics=("arbitrary", "arbitrary"), vmem_limit_bytes=VMEM_LIMIT),
        name="diff_attn",
    )(slopes, qt, k, vt, zt, lam_p, jnp.broadcast_to(g_sub[:, None], (V_DIM, tq)))


def _attn_out_kernel(x_ref, gt_ref, w_ref, g_ref, y_ref):
    y = lax.dot_general(gt_ref[0], w_ref[...], (((0,), (0,)), ((), ())),
                        preferred_element_type=jnp.float32)
    y_ref[0] = x_ref[0] + _rms(y, g_ref[...])


def _attn_out(x, gt, w_out, g_post, *, tm=512):
    bsz, seq, d = x.shape
    width = gt.shape[1]
    tile = pl.BlockSpec((1, tm, d), lambda b, t: (b, t, 0))
    return pl.pallas_call(
        _attn_out_kernel,
        grid=(bsz, seq // tm),
        in_specs=[tile, pl.BlockSpec((1, width, tm), lambda b, t: (b, 0, t)),
                  pl.BlockSpec((width, d), lambda b, t: (0, 0)),
                  pl.BlockSpec((1, d), lambda b, t: (0, 0))],
        out_specs=tile,
        out_shape=jax.ShapeDtypeStruct(x.shape, jnp.float32),
        compiler_params=pltpu.CompilerParams(
            dimension_semantics=("arbitrary", "arbitrary"), vmem_limit_bytes=VMEM_LIMIT),
        name="attn_out",
    )(x, gt, w_out.astype(jnp.bfloat16), g_post.reshape(1, -1))


def kernel(x, a_g_pre, a_w_in, a_w_dw, a_b_dw, a_ln_g, a_ln_b, a_w_out, a_g_post,
           kv_g, w_kv, b_g_pre, b_w_in, b_lambda, b_g_sub, b_w_out, b_g_post):
    n_a, n_b = a_w_in.shape[0], b_w_in.shape[0]
    width = b_w_out.shape[1]
    n_heads = width // V_DIM
    assert w_kv.shape[1] == 2 * width and b_w_in.shape[2] == 2 * width

    for l in range(n_a):
        x = _conv_layer(x, a_g_pre[l], a_w_in[l], a_w_dw[l], a_b_dw[l], a_ln_g[l], a_ln_b[l],
                        a_w_out[l], a_g_post[l])

    k = vt = None
    for j in range(n_b):
        q_out = (0, True, HEAD_DIM ** -0.5)
        z_out = (0, True, 1.0)
        w_q, w_z = b_w_in[j][:, :width].T, b_w_in[j][:, width:].T
        if j == 0:
            gains = jnp.stack([b_g_pre[j], kv_g])
            qt, zt, k, vt = _proj(
                x, gains, [w_q, w_z, w_kv[:, :width], w_kv[:, width:].T],
                [q_out, z_out, (1, False, 1.0), (1, True, 1.0)],
                [jnp.bfloat16, jnp.float32, jnp.bfloat16, jnp.bfloat16])
        else:
            qt, zt = _proj(x, b_g_pre[j][None], [w_q, w_z], [q_out, z_out],
                           [jnp.bfloat16, jnp.float32])
        lam_init = 0.8 - 0.6 * math.exp(-0.3 * (n_a + j + 1))
        gt = _attention(qt, k, vt, zt, b_lambda[j], b_g_sub[j], n_heads=n_heads, lam_init=lam_init)
        x = _attn_out(x, gt, b_w_out[j], b_g_post[j])
    return x
```

```python
import functools
import math

import jax
import jax.numpy as jnp
from jax import lax
from jax.experimental import pallas as pl
from jax.experimental.pallas import tpu as pltpu

EPS = 1e-6
HEAD_DIM = 64
V_DIM = 2 * HEAD_DIM
CONV_HALO = 32
CONV_ROWS = 64
LANES = 128
SUBLANES = 8
POS_SPLIT = 256
POS_COLS = 6
ONES_ROWS = 16
LOG2E = 1.4426950408889634
ATTN_SLOTS = 4
ATTN_LAG = 16
NEG_BIG = -1e30
VMEM_LIMIT = 56 * 1024 * 1024


def _rms(x, g):
    return x * lax.rsqrt(jnp.mean(x * x, axis=-1, keepdims=True) + EPS) * g


def _sigmoid(x):
    return 0.5 * jnp.tanh(0.5 * x) + 0.5


def _conv_layer_kernel(x_ref, gpre_ref, win_ref, wdw_ref, bdw_ref, lng_ref, lnb_ref,
                       wout_ref, gpost_ref, o_ref, cbuf_ref, ybuf_ref, *, ts, width, taps):
    t = pl.program_id(1)
    n_slab = width // LANES
    x = x_ref[0]
    h = _rms(x, gpre_ref[...]).astype(jnp.bfloat16)
    u = jnp.dot(h, win_ref[...], preferred_element_type=jnp.float32)
    a = u[:, :width]
    b = u[:, width:2 * width]
    z = u[:, 2 * width:]

    @pl.when(t == 0)
    def _():
        cbuf_ref[:, 0:CONV_HALO, :] = jnp.zeros((n_slab, CONV_HALO, LANES), jnp.float32)

    @pl.when(t > 0)
    def _():
        cbuf_ref[:, 0:CONV_HALO, :] = cbuf_ref[:, ts:ts + CONV_HALO, :]

    c = a * _sigmoid(b)
    for j in range(n_slab):
        cbuf_ref[j, CONV_HALO:CONV_HALO + ts, :] = c[:, j * LANES:(j + 1) * LANES]

    first = CONV_HALO - (taps - 1)
    for j in range(n_slab):
        w_rows = [jnp.broadcast_to(wdw_ref[j, k:k + 1, :], (SUBLANES, LANES)) for k in range(taps)]
        bias = jnp.broadcast_to(bdw_ref[j], (CONV_ROWS, LANES))

        def chunk(i, carry, j=j, w_rows=w_rows, bias=bias):
            r0 = pl.multiple_of(i * CONV_ROWS, CONV_ROWS)
            acc = bias
            for k in range(taps):
                wk = jnp.concatenate([w_rows[k]] * (CONV_ROWS // SUBLANES), axis=0)
                acc = acc + wk * cbuf_ref[j, pl.ds(r0 + (first + k), CONV_ROWS), :]
            ybuf_ref[j, pl.ds(r0, CONV_ROWS), :] = acc
            return carry

        lax.fori_loop(0, ts // CONV_ROWS, chunk, 0, unroll=2)

    y = jnp.concatenate([ybuf_ref[j] for j in range(n_slab)], axis=1)
    mu = jnp.mean(y, axis=-1, keepdims=True)
    dev = y - mu
    var = jnp.mean(dev * dev, axis=-1, keepdims=True)
    y = dev * lax.rsqrt(var + EPS) * lng_ref[...] + lnb_ref[...]
    y = y * _sigmoid(y)
    gated = (y * (z * _sigmoid(z))).astype(jnp.bfloat16)
    y = jnp.dot(gated, wout_ref[...], preferred_element_type=jnp.float32)
    o_ref[0] = x + _rms(y, gpost_ref[...])


def _conv_layer(x, g_pre, w_in, w_dw, b_dw, ln_g, ln_b, w_out, g_post, *, ts=512):
    bsz, seq, d = x.shape
    taps, width = w_dw.shape
    n_slab = width // LANES
    assert taps - 1 <= CONV_HALO and seq % ts == 0 and ts % (2 * CONV_ROWS) == 0
    row = lambda v: v.reshape(1, -1)
    full = lambda shape: pl.BlockSpec(shape, lambda b, t: (0,) * len(shape))
    w_dw_slabs = w_dw.reshape(taps, n_slab, LANES).transpose(1, 0, 2)
    b_dw_slabs = b_dw.reshape(n_slab, 1, LANES)
    return pl.pallas_call(
        functools.partial(_conv_layer_kernel, ts=ts, width=width, taps=taps),
        grid=(bsz, seq // ts),
        in_specs=[
            pl.BlockSpec((1, ts, d), lambda b, t: (b, t, 0)),
            full((1, d)), full((d, 3 * width)), full((n_slab, taps, LANES)),
            full((n_slab, 1, LANES)), full((1, width)), full((1, width)),
            full((width, d)), full((1, d)),
        ],
        out_specs=pl.BlockSpec((1, ts, d), lambda b, t: (b, t, 0)),
        out_shape=jax.ShapeDtypeStruct(x.shape, jnp.float32),
        scratch_shapes=[
            pltpu.VMEM((n_slab, CONV_HALO + ts, LANES), jnp.float32),
            pltpu.VMEM((n_slab, ts, LANES), jnp.float32),
        ],
        compiler_params=pltpu.CompilerParams(
            dimension_semantics=("arbitrary", "arbitrary"), vmem_limit_bytes=VMEM_LIMIT),
        name="conv_layer",
    )(x, row(g_pre), w_in.astype(jnp.bfloat16), w_dw_slabs, b_dw_slabs, row(ln_g), row(ln_b),
      w_out.astype(jnp.bfloat16), row(g_post))


def _proj_kernel(x_ref, gains_ref, *refs, outs):
    w_refs, o_refs = refs[:len(outs)], refs[len(outs):]
    x = x_ref[0]
    xn = x * lax.rsqrt(jnp.mean(x * x, axis=-1, keepdims=True) + EPS)
    hs = {}
    for (g, feature_major, scale), w_ref, o_ref in zip(outs, w_refs, o_refs):
        if g not in hs:
            hs[g] = (xn * gains_ref[g:g + 1, :]).astype(jnp.bfloat16)
        if feature_major:
            u = lax.dot_general(w_ref[...], hs[g], (((1,), (1,)), ((), ())),
                                preferred_element_type=jnp.float32)
        else:
            u = jnp.dot(hs[g], w_ref[...], preferred_element_type=jnp.float32)
        if scale != 1.0:
            u = u * scale
        o_ref[0] = u.astype(o_ref.dtype)


def _proj(x, gains, weights, outs, dtypes, *, tm=512):
    bsz, seq, d = x.shape
    in_specs = [pl.BlockSpec((1, tm, d), lambda b, t: (b, t, 0)),
                pl.BlockSpec(gains.shape, lambda b, t: (0, 0))]
    out_specs, out_shapes = [], []
    for (g, feature_major, scale), w, dt in zip(outs, weights, dtypes):
        in_specs.append(pl.BlockSpec(w.shape, lambda b, t: (0, 0)))
        if feature_major:
            cols = w.shape[0]
            out_specs.append(pl.BlockSpec((1, cols, tm), lambda b, t: (b, 0, t)))
            out_shapes.append(jax.ShapeDtypeStruct((bsz, cols, seq), dt))
        else:
            cols = w.shape[1]
            out_specs.append(pl.BlockSpec((1, tm, cols), lambda b, t: (b, t, 0)))
            out_shapes.append(jax.ShapeDtypeStruct((bsz, seq, cols), dt))
    return pl.pallas_call(
        functools.partial(_proj_kernel, outs=tuple(outs)),
        grid=(bsz, seq // tm),
        in_specs=in_specs, out_specs=out_specs, out_shape=out_shapes,
        compiler_params=pltpu.CompilerParams(
            dimension_semantics=("arbitrary", "arbitrary"), vmem_limit_bytes=VMEM_LIMIT),
        name="proj",
    )(x, gains, *[w.astype(jnp.bfloat16) for w in weights])


def _attn_schedule(tiles_per_seq, slots, lag):
    flat = [(n, kb) for n, tiles in enumerate(tiles_per_seq) for kb in range(tiles)]
    done = [set(), set(), set()]
    ptr = [0, 0, 0]
    order = []

    def seq_done(stage, n):
        return n < 0 or all((n, kb) in done[stage] for kb in range(tiles_per_seq[n]))

    def ready(stage):
        if ptr[stage] == len(flat):
            return False
        n, kb = flat[ptr[stage]]
        if stage == 0:
            return seq_done(1, n - slots)
        if stage == 1:
            return seq_done(0, n) and seq_done(2, n - slots)
        return (n, kb) in done[1]

    while ptr[2] < len(flat):
        feeder_stuck = False
        for stage in range(3):
            ok = ready(stage)
            if ok and stage > 0:
                feeder_ended = ptr[stage - 1] == len(flat)
                ok = ptr[stage - 1] - ptr[stage] > lag or feeder_stuck or feeder_ended
            if ok:
                item = flat[ptr[stage]]
                order.append((stage, *item))
                done[stage].add(item)
                ptr[stage] += 1
            feeder_stuck = not ok and ptr[stage] < len(flat)
    return order


def _attn_kernel(slope_ref, qt_ref, k_ref, vt_ref, zt_ref, lam_ref, gsub_ref, gt_ref,
                 kaug_ref, vaug_ref, s_ref, p_ref, *, tq, lam_init):
    first = (pl.program_id(0) == 0) & (pl.program_id(1) == 0)
    seq = k_ref.shape[1]
    n_slots = s_ref.shape[0]
    slope = slope_ref[pl.program_id(1)]

    @pl.when(first)
    def _():
        j = lax.broadcasted_iota(jnp.int32, (seq, V_DIM), 0)
        lane = lax.broadcasted_iota(jnp.int32, (seq, V_DIM), 1)
        digit = jnp.where(lane % 2 == 0, j // POS_SPLIT, j % POS_SPLIT)
        kaug_ref[:, V_DIM:] = jnp.where(lane < POS_COLS, digit, 0).astype(jnp.float32).astype(jnp.bfloat16)
        vaug_ref[V_DIM:, :] = jnp.ones((ONES_ROWS, seq), jnp.bfloat16)

    kaug_ref[:, :V_DIM] = k_ref[0]
    vaug_ref[:V_DIM, :] = vt_ref[0]

    r = lax.broadcasted_iota(jnp.int32, (ONES_ROWS, tq), 0)
    c0 = jnp.full((ONES_ROWS, tq), slope * LOG2E, jnp.float32)
    c1 = c0.astype(jnp.bfloat16).astype(jnp.float32)
    c2 = (c0 - c1).astype(jnp.bfloat16).astype(jnp.float32)
    piece = jnp.where(r < 2, c1, jnp.where(r < 4, c2, c0 - c1 - c2))
    pos_rows = jnp.where(r < POS_COLS, jnp.where(r % 2 == 0, piece * POS_SPLIT, piece), 0.0)
    pos_rows = pos_rows.astype(jnp.bfloat16)
    zeros_half = jnp.zeros((HEAD_DIM, tq), jnp.bfloat16)
    zeros_tail = jnp.zeros((V_DIM - ONES_ROWS, tq), jnp.bfloat16)
    key_i = lax.broadcasted_iota(jnp.int32, (tq, tq), 0)
    qry_i = lax.broadcasted_iota(jnp.int32, (tq, tq), 1)
    causal = key_i <= qry_i

    lp = lam_ref[...]
    lam = (jnp.exp(jnp.sum(lp[0:1] * lp[1:2], axis=-1, keepdims=True))
           - jnp.exp(jnp.sum(lp[2:3] * lp[3:4], axis=-1, keepdims=True)) + lam_init)

    seqs = [(qi, c) for qi in range(seq // tq) for c in range(2)]
    state = {}

    def score_tile(n, kb):
        qi, c = seqs[n]
        if kb == 0:
            qt = qt_ref[0, :, qi * tq:(qi + 1) * tq]
            halves = (qt[:HEAD_DIM], zeros_half) if c == 0 else (zeros_half, qt[HEAD_DIM:])
            state[n] = dict(qa=jnp.concatenate([*halves, pos_rows, zeros_tail], axis=0), mx=None)
        st = state[n]
        rows = slice(kb * tq, (kb + 1) * tq)
        s = jnp.dot(kaug_ref[rows, :], st["qa"], preferred_element_type=jnp.float32)
        if kb == qi:
            s = jnp.where(causal, s, NEG_BIG)
        s_ref[n % n_slots, rows, :] = s
        part = jnp.max(s.reshape(tq // SUBLANES, SUBLANES, tq), axis=0)
        st["mx"] = part if st["mx"] is None else jnp.maximum(st["mx"], part)
        if kb == qi:
            st["m"] = jnp.max(st["mx"], axis=0, keepdims=True)

    def prob_tile(n, kb):
        rows = slice(kb * tq, (kb + 1) * tq)
        p = jnp.exp2(s_ref[n % n_slots, rows, :] - state[n]["m"])
        p_ref[n % n_slots, rows, :] = p.astype(jnp.bfloat16)

    def value_tile(n, kb):
        qi, c = seqs[n]
        st = state[n]
        rows = slice(kb * tq, (kb + 1) * tq)
        part = jnp.dot(vaug_ref[:, rows], p_ref[n % n_slots, rows, :],
                       preferred_element_type=jnp.float32)
        st["acc"] = part if kb == 0 else st["acc"] + part
        if kb < qi:
            return
        st["out"] = st["acc"][:V_DIM] * (1.0 / st["acc"][V_DIM:V_DIM + 1])
        if c == 1:
            o = state[n - 1]["out"] - lam * st["out"]
            ms = jnp.mean(o * o, axis=0, keepdims=True)
            o = o * lax.rsqrt(ms + EPS) * gsub_ref[...] * (1.0 - lam_init)
            cols = slice(qi * tq, (qi + 1) * tq)
            z = zt_ref[0, :, cols]
            gt_ref[0, :, cols] = (o * (z * _sigmoid(z))).astype(jnp.bfloat16)

    stages = (score_tile, prob_tile, value_tile)
    for stage, n, kb in _attn_schedule([qi + 1 for qi, _ in seqs], n_slots, ATTN_LAG):
        stages[stage](n, kb)


def _attention(qt, k, vt, zt, lam_p, g_sub, *, n_heads, lam_init, tq=256):
    bsz, seq, _ = k.shape
    assert seq % tq == 0 and seq <= POS_SPLIT * POS_SPLIT
    slopes = jnp.exp2(-8.0 * jnp.arange(1, n_heads + 1, dtype=jnp.float32) / n_heads)
    feat = pl.BlockSpec((1, V_DIM, seq), lambda b, h, s: (b, h, 0))
    grid_spec = pltpu.PrefetchScalarGridSpec(
        num_scalar_prefetch=1,
        grid=(bsz, n_heads),
        in_specs=[
            feat,
            pl.BlockSpec((1, seq, V_DIM), lambda b, h, s: (b, 0, h)),
            feat, feat,
            pl.BlockSpec((4, HEAD_DIM), lambda b, h, s: (0, 0)),
            pl.BlockSpec((V_DIM, tq), lambda b, h, s: (0, 0)),
        ],
        out_specs=feat,
        scratch_shapes=[
            pltpu.VMEM((seq, 2 * V_DIM), jnp.bfloat16),
            pltpu.VMEM((V_DIM + ONES_ROWS, seq), jnp.bfloat16),
            pltpu.VMEM((ATTN_SLOTS, seq, tq), jnp.float32),
            pltpu.VMEM((ATTN_SLOTS, seq, tq), jnp.bfloat16),
        ],
    )
    return pl.pallas_call(
        functools.partial(_attn_kernel, tq=tq, lam_init=lam_init),
        grid_spec=grid_spec,
        out_shape=jax.ShapeDtypeStruct((bsz, n_heads * V_DIM, seq), jnp.bfloat16),
        compiler_params=pltpu.CompilerParams(
            dimension_semantics=("arbitrary", "arbitrary"), vmem_limit_bytes=VMEM_LIMIT),
        name="diff_attn",
    )(slopes, qt, k, vt, zt, lam_p, jnp.broadcast_to(g_sub[:, None], (V_DIM, tq)))


def _attn_out_kernel(x_ref, gt_ref, w_ref, g_ref, y_ref):
    y = lax.dot_general(gt_ref[0], w_ref[...], (((0,), (0,)), ((), ())),
                        preferred_element_type=jnp.float32)
    y_ref[0] = x_ref[0] + _rms(y, g_ref[...])


def _attn_out(x, gt, w_out, g_post, *, tm=512):
    bsz, seq, d = x.shape
    width = gt.shape[1]
    tile = pl.BlockSpec((1, tm, d), lambda b, t: (b, t, 0))
    return pl.pallas_call(
        _attn_out_kernel,
        grid=(bsz, seq // tm),
        in_specs=[tile, pl.BlockSpec((1, width, tm), lambda b, t: (b, 0, t)),
                  pl.BlockSpec((width, d), lambda b, t: (0, 0)),
                  pl.BlockSpec((1, d), lambda b, t: (0, 0))],
        out_specs=tile,
        out_shape=jax.ShapeDtypeStruct(x.shape, jnp.float32),
        compiler_params=pltpu.CompilerParams(
            dimension_semantics=("arbitrary", "arbitrary"), vmem_limit_bytes=VMEM_LIMIT),
        name="attn_out",
    )(x, gt, w_out.astype(jnp.bfloat16), g_post.reshape(1, -1))


def kernel(x, a_g_pre, a_w_in, a_w_dw, a_b_dw, a_ln_g, a_ln_b, a_w_out, a_g_post,
           kv_g, w_kv, b_g_pre, b_w_in, b_lambda, b_g_sub, b_w_out, b_g_post):
    n_a, n_b = a_w_in.shape[0], b_w_in.shape[0]
    width = b_w_out.shape[1]
    n_heads = width // V_DIM
    assert w_kv.shape[1] == 2 * width and b_w_in.shape[2] == 2 * width

    for l in range(n_a):
        x = _conv_layer(x, a_g_pre[l], a_w_in[l], a_w_dw[l], a_b_dw[l], a_ln_g[l], a_ln_b[l],
                        a_w_out[l], a_g_post[l])

    k = vt = None
    for j in range(n_b):
        q_out = (0, True, HEAD_DIM ** -0.5 * LOG2E)
        z_out = (0, True, 1.0)
        w_q, w_z = b_w_in[j][:, :width].T, b_w_in[j][:, width:].T
        if j == 0:
            gains = jnp.stack([b_g_pre[j], kv_g])
            qt, zt, k, vt = _proj(
                x, gains, [w_q, w_z, w_kv[:, :width], w_kv[:, width:].T],
                [q_out, z_out, (1, False, 1.0), (1, True, 1.0)],
                [jnp.bfloat16, jnp.float32, jnp.bfloat16, jnp.bfloat16])
        else:
            qt, zt = _proj(x, b_g_pre[j][None], [w_q, w_z], [q_out, z_out],
                           [jnp.bfloat16, jnp.float32])
        lam_init = 0.8 - 0.6 * math.exp(-0.3 * (n_a + j + 1))
        gt = _attention(qt, k, vt, zt, b_lambda[j], b_g_sub[j], n_heads=n_heads, lam_init=lam_init)
        x = _attn_out(x, gt, b_w_out[j], b_g_post[j])
    return x
```

```python
import functools
import math

import jax
import jax.numpy as jnp
from jax import lax
from jax.experimental import pallas as pl
from jax.experimental.pallas import tpu as pltpu

EPS = 1e-6
HEAD_DIM = 64
V_DIM = 2 * HEAD_DIM
CONV_HALO = 32
CONV_ROWS = 64
LANES = 128
SUBLANES = 8
MXU_COLS = 256
POS_SPLIT = 256
POS_COLS = 6
ONES_ROWS = 16
LOG2E = 1.4426950408889634
Q_SCALE = HEAD_DIM ** -0.5 * LOG2E
ATTN_SLOTS = 4
ATTN_LAG = 16
NEG_BIG = -1e30
VMEM_LIMIT = 56 * 1024 * 1024

_NT = (((1,), (1,)), ((), ()))


def _rms(x, g):
    return x * lax.rsqrt(jnp.mean(x * x, axis=-1, keepdims=True) + EPS) * g


def _sigmoid(x):
    return 0.5 * jnp.tanh(0.5 * x) + 0.5


def _interleave(xs, ys):
    out, i, j = [], 0, 0
    while i < len(xs) or j < len(ys):
        if j == len(ys) or (i < len(xs) and i * len(ys) <= j * len(xs)):
            out.append(xs[i])
            i += 1
        else:
            out.append(ys[j])
            j += 1
    return out


def _conv_block_kernel(*refs, ts, width, taps, with_proj):
    (x_ref, gpre_ref, win_ref, wdw_ref, bdw_ref, lng_ref, lnb_ref, wout_ref, gpost_ref) = refs[:9]
    refs = refs[9:]
    if with_proj:
        gains_ref, wq_ref, wz_ref, wk_ref, wv_ref = refs[:5]
        o_ref, qt_ref, zt_ref, k_ref, vt_ref = refs[5:10]
        refs = refs[10:]
    else:
        o_ref = refs[0]
        refs = refs[1:]
    cbuf_ref, zbuf_ref, ybuf_ref, gbuf_ref, y2buf_ref, hbuf_ref = refs
    t = pl.program_id(1)
    n_slab = width // LANES
    d = x_ref.shape[-1]
    first = CONV_HALO - (taps - 1)

    @pl.when(t == 0)
    def _():
        cbuf_ref[:, :, 0:CONV_HALO, :] = jnp.zeros((2, n_slab, CONV_HALO, LANES), jnp.float32)

    @pl.when(t > 0)
    def _():
        cbuf_ref[:, :, 0:CONV_HALO, :] = cbuf_ref[:, :, ts:ts + CONV_HALO, :]

    def in_glu(s, j):
        if j == 0:
            hbuf_ref[s, 0] = _rms(x_ref[s, 0], gpre_ref[...]).astype(jnp.bfloat16)
        h = hbuf_ref[s, 0]
        lo = j * MXU_COLS
        a = jnp.dot(h, win_ref[:, lo:lo + MXU_COLS], preferred_element_type=jnp.float32)
        b = jnp.dot(h, win_ref[:, width + lo:width + lo + MXU_COLS],
                    preferred_element_type=jnp.float32)
        c = a * _sigmoid(b)
        for q in range(MXU_COLS // LANES):
            cbuf_ref[s, j * (MXU_COLS // LANES) + q, CONV_HALO:CONV_HALO + ts, :] = (
                c[:, q * LANES:(q + 1) * LANES])

    def in_gate(s, j):
        lo = j * MXU_COLS
        z = jnp.dot(hbuf_ref[s, 0], win_ref[:, 2 * width + lo:2 * width + lo + MXU_COLS],
                    preferred_element_type=jnp.float32)
        zbuf_ref[s, :, lo:lo + MXU_COLS] = z * _sigmoid(z)

    def conv(s, j, i):
        r0 = i * CONV_ROWS
        acc = jnp.broadcast_to(bdw_ref[j], (CONV_ROWS, LANES))
        for k in range(taps):
            acc = acc + wdw_ref[j, k:k + 1, :] * cbuf_ref[s, j, r0 + first + k:r0 + first + k + CONV_ROWS, :]
        ybuf_ref[s, j, r0:r0 + CONV_ROWS, :] = acc

    def norm_gate(s, i):
        rows = slice(i * CONV_ROWS, (i + 1) * CONV_ROWS)
        y = jnp.concatenate([ybuf_ref[s, j, rows, :] for j in range(n_slab)], axis=1)
        mu = jnp.mean(y, axis=-1, keepdims=True)
        dev = y - mu
        var = jnp.mean(dev * dev, axis=-1, keepdims=True)
        y = dev * lax.rsqrt(var + EPS) * lng_ref[...] + lnb_ref[...]
        y = y * _sigmoid(y)
        gbuf_ref[s, rows, :] = (y * zbuf_ref[s, rows, :]).astype(jnp.bfloat16)

    def out_proj(s, n):
        lo = n * MXU_COLS
        y2buf_ref[s, :, lo:lo + MXU_COLS] = jnp.dot(
            gbuf_ref[s], wout_ref[:, lo:lo + MXU_COLS], preferred_element_type=jnp.float32)

    def residual(s):
        x1 = x_ref[s, 0] + _rms(y2buf_ref[s], gpost_ref[...])
        o_ref[s, 0] = x1
        if with_proj:
            xn = x1 * lax.rsqrt(jnp.mean(x1 * x1, axis=-1, keepdims=True) + EPS)
            hbuf_ref[s, 0] = (xn * gains_ref[0:1, :]).astype(jnp.bfloat16)
            hbuf_ref[s, 1] = (xn * gains_ref[1:2, :]).astype(jnp.bfloat16)

    def proj(s, which, n):
        cols = slice(n * MXU_COLS, (n + 1) * MXU_COLS)
        if which == 0:
            u = lax.dot_general(wq_ref[cols, :], hbuf_ref[s, 0], _NT, preferred_element_type=jnp.float32)
            qt_ref[s, 0, cols, :] = (u * Q_SCALE).astype(jnp.bfloat16)
        elif which == 1:
            zt_ref[s, 0, cols, :] = lax.dot_general(wz_ref[cols, :], hbuf_ref[s, 0], _NT,
                                                    preferred_element_type=jnp.float32)
        elif which == 2:
            u = jnp.dot(hbuf_ref[s, 1], wk_ref[:, cols], preferred_element_type=jnp.float32)
            k_ref[s, 0, :, cols] = u.astype(jnp.bfloat16)
        else:
            u = lax.dot_general(wv_ref[cols, :], hbuf_ref[s, 1], _NT, preferred_element_type=jnp.float32)
            vt_ref[s, 0, cols, :] = u.astype(jnp.bfloat16)

    task = functools.partial
    n_chunks = width // MXU_COLS

    def front(s):
        return ([task(in_glu, s, j) for j in range(n_chunks)]
                + [task(in_gate, s, j) for j in range(n_chunks)])

    def convs(s):
        return [task(conv, s, j, i) for j in range(n_slab) for i in range(ts // CONV_ROWS)]

    def back(s):
        tasks = ([task(norm_gate, s, i) for i in range(ts // CONV_ROWS)]
                 + [task(out_proj, s, n) for n in range(d // MXU_COLS)] + [task(residual, s)])
        if with_proj:
            tasks += [task(proj, s, which, n) for which in range(4) for n in range(n_chunks)]
        return tasks

    order = front(0) + _interleave(front(1), convs(0)) + _interleave(back(0), convs(1)) + back(1)
    for run in order:
        run()


def _conv_block(x, g_pre, w_in, w_dw, b_dw, ln_g, ln_b, w_out, g_post, proj=None, *, ts=256):
    bsz, seq, d = x.shape
    taps, width = w_dw.shape
    n_slab = width // LANES
    hb = bsz // 2
    assert bsz % 2 == 0 and taps - 1 <= CONV_HALO and seq % ts == 0 and ts % CONV_ROWS == 0
    assert width % MXU_COLS == 0 and d % MXU_COLS == 0
    row = lambda v: v.reshape(1, -1)
    bf16 = lambda w: w.astype(jnp.bfloat16)
    full = lambda a: pl.BlockSpec(a.shape, lambda b, t: (0,) * a.ndim, pipeline_mode=pl.Buffered(1))
    pair = lambda r, c: pl.BlockSpec((2, 1, r, c), lambda b, t: (0, b, t, 0))
    pair_t = lambda r, c: pl.BlockSpec((2, 1, r, c), lambda b, t: (0, b, 0, t))
    w_dw_slabs = w_dw.reshape(taps, n_slab, LANES).transpose(1, 0, 2)
    b_dw_slabs = b_dw.reshape(n_slab, 1, LANES)
    args = [x.reshape(2, hb, seq, d), row(g_pre), bf16(w_in), w_dw_slabs, b_dw_slabs, row(ln_g),
            row(ln_b), bf16(w_out), row(g_post)]
    in_specs = [pair(ts, d)] + [full(a) for a in args[1:]]
    out_specs = [pair(ts, d)]
    out_shapes = [jax.ShapeDtypeStruct((2, hb, seq, d), jnp.float32)]
    if proj is not None:
        gains, wq_t, wz_t, wk, wv_t = proj
        extra = [gains, bf16(wq_t), bf16(wz_t), bf16(wk), bf16(wv_t)]
        args += extra
        in_specs += [full(a) for a in extra]
        out_specs += [pair_t(width, ts), pair_t(width, ts), pair(ts, width), pair_t(width, ts)]
        out_shapes += [jax.ShapeDtypeStruct((2, hb, width, seq), jnp.bfloat16),
                       jax.ShapeDtypeStruct((2, hb, width, seq), jnp.float32),
                       jax.ShapeDtypeStruct((2, hb, seq, width), jnp.bfloat16),
                       jax.ShapeDtypeStruct((2, hb, width, seq), jnp.bfloat16)]
    outs = pl.pallas_call(
        functools.partial(_conv_block_kernel, ts=ts, width=width, taps=taps,
                          with_proj=proj is not None),
        grid=(hb, seq // ts),
        in_specs=in_specs, out_specs=out_specs, out_shape=out_shapes,
        scratch_shapes=[
            pltpu.VMEM((2, n_slab, CONV_HALO + ts, LANES), jnp.float32),
            pltpu.VMEM((2, ts, width), jnp.float32),
            pltpu.VMEM((2, n_slab, ts, LANES), jnp.float32),
            pltpu.VMEM((2, ts, width), jnp.bfloat16),
            pltpu.VMEM((2, ts, d), jnp.float32),
            pltpu.VMEM((2, 2, ts, d), jnp.bfloat16),
        ],
        compiler_params=pltpu.CompilerParams(
            dimension_semantics=("arbitrary", "arbitrary"), vmem_limit_bytes=VMEM_LIMIT),
        name="conv_block",
    )(*args)
    return [o.reshape(bsz, *o.shape[2:]) for o in outs]


def _proj_kernel(x_ref, gain_ref, wq_ref, wz_ref, qt_ref, zt_ref):
    h = _rms(x_ref[0], gain_ref[...]).astype(jnp.bfloat16)
    u = lax.dot_general(wq_ref[...], h, _NT, preferred_element_type=jnp.float32)
    qt_ref[0] = (u * Q_SCALE).astype(jnp.bfloat16)
    zt_ref[0] = lax.dot_general(wz_ref[...], h, _NT, preferred_element_type=jnp.float32)


def _proj(x, gain, wq_t, wz_t, *, tm=512):
    bsz, seq, d = x.shape
    width = wq_t.shape[0]
    feat = pl.BlockSpec((1, width, tm), lambda b, t: (b, 0, t))
    whole = lambda a: pl.BlockSpec(a.shape, lambda b, t: (0,) * a.ndim)
    args = [x, gain.reshape(1, -1), wq_t.astype(jnp.bfloat16), wz_t.astype(jnp.bfloat16)]
    return pl.pallas_call(
        _proj_kernel,
        grid=(bsz, seq // tm),
        in_specs=[pl.BlockSpec((1, tm, d), lambda b, t: (b, t, 0))] + [whole(a) for a in args[1:]],
        out_specs=[feat, feat],
        out_shape=[jax.ShapeDtypeStruct((bsz, width, seq), jnp.bfloat16),
                   jax.ShapeDtypeStruct((bsz, width, seq), jnp.float32)],
        compiler_params=pltpu.CompilerParams(
            dimension_semantics=("arbitrary", "arbitrary"), vmem_limit_bytes=VMEM_LIMIT),
        name="proj",
    )(*args)


def _attn_schedule(tiles_per_seq, slots, lag):
    flat = [(n, kb) for n, tiles in enumerate(tiles_per_seq) for kb in range(tiles)]
    done = [set(), set(), set()]
    ptr = [0, 0, 0]
    order = []

    def seq_done(stage, n):
        return n < 0 or all((n, kb) in done[stage] for kb in range(tiles_per_seq[n]))

    def ready(stage):
        if ptr[stage] == len(flat):
            return False
        n, kb = flat[ptr[stage]]
        if stage == 0:
            return seq_done(1, n - slots)
        if stage == 1:
            return seq_done(0, n) and seq_done(2, n - slots)
        return (n, kb) in done[1]

    while ptr[2] < len(flat):
        feeder_stuck = False
        for stage in range(3):
            ok = ready(stage)
            if ok and stage > 0:
                feeder_ended = ptr[stage - 1] == len(flat)
                ok = ptr[stage - 1] - ptr[stage] > lag or feeder_stuck or feeder_ended
            if ok:
                item = flat[ptr[stage]]
                order.append((stage, *item))
                done[stage].add(item)
                ptr[stage] += 1
            feeder_stuck = not ok and ptr[stage] < len(flat)
    return order


def _attn_kernel(slope_ref, qt_ref, k_ref, vt_ref, zt_ref, lam_ref, gsub_ref, gt_ref,
                 kaug_ref, vaug_ref, s_ref, p_ref, *, tq, lam_init):
    first = (pl.program_id(0) == 0) & (pl.program_id(1) == 0)
    n_heads = kaug_ref.shape[0]
    seq = k_ref.shape[1]
    n_slots = s_ref.shape[0]

    @pl.when(first)
    def _():
        j = lax.broadcasted_iota(jnp.int32, (seq, V_DIM), 0)
        lane = lax.broadcasted_iota(jnp.int32, (seq, V_DIM), 1)
        digit = jnp.where(lane % 2 == 0, j // POS_SPLIT, j % POS_SPLIT)
        digits = jnp.where(lane < POS_COLS, digit, 0).astype(jnp.float32).astype(jnp.bfloat16)
        for hh in range(n_heads):
            kaug_ref[hh, :, V_DIM:] = digits
            vaug_ref[hh, V_DIM:, :] = jnp.ones((ONES_ROWS, seq), jnp.bfloat16)

    for hh in range(n_heads):
        kaug_ref[hh, :, :V_DIM] = k_ref[0, :, hh * V_DIM:(hh + 1) * V_DIM]
        vaug_ref[hh, :V_DIM, :] = vt_ref[0, hh * V_DIM:(hh + 1) * V_DIM, :]

    r = lax.broadcasted_iota(jnp.int32, (ONES_ROWS, tq), 0)
    pos_rows = []
    for hh in range(n_heads):
        slope = slope_ref[pl.program_id(1) * n_heads + hh]
        c0 = jnp.full((ONES_ROWS, tq), slope * LOG2E, jnp.float32)
        c1 = c0.astype(jnp.bfloat16).astype(jnp.float32)
        c2 = (c0 - c1).astype(jnp.bfloat16).astype(jnp.float32)
        piece = jnp.where(r < 2, c1, jnp.where(r < 4, c2, c0 - c1 - c2))
        rows = jnp.where(r < POS_COLS, jnp.where(r % 2 == 0, piece * POS_SPLIT, piece), 0.0)
        pos_rows.append(rows.astype(jnp.bfloat16))
    zeros_half = jnp.zeros((HEAD_DIM, tq), jnp.bfloat16)
    zeros_tail = jnp.zeros((V_DIM - ONES_ROWS, tq), jnp.bfloat16)
    key_i = lax.broadcasted_iota(jnp.int32, (tq, tq), 0)
    qry_i = lax.broadcasted_iota(jnp.int32, (tq, tq), 1)
    causal = key_i <= qry_i

    lp = lam_ref[...]
    lam = (jnp.exp(jnp.sum(lp[0:1] * lp[1:2], axis=-1, keepdims=True))
           - jnp.exp(jnp.sum(lp[2:3] * lp[3:4], axis=-1, keepdims=True)) + lam_init)

    seqs = [(qi, hh, c) for qi in range(seq // tq) for hh in range(n_heads) for c in range(2)]
    state = {}

    def score_tile(n, kb):
        qi, hh, c = seqs[n]
        if kb == 0:
            qt = qt_ref[0, hh * V_DIM:(hh + 1) * V_DIM, qi * tq:(qi + 1) * tq]
            halves = (qt[:HEAD_DIM], zeros_half) if c == 0 else (zeros_half, qt[HEAD_DIM:])
            state[n] = dict(qa=jnp.concatenate([*halves, pos_rows[hh], zeros_tail], axis=0), mx=None)
        st = state[n]
        rows = slice(kb * tq, (kb + 1) * tq)
        s = jnp.dot(kaug_ref[hh, rows, :], st["qa"], preferred_element_type=jnp.float32)
        if kb == qi:
            s = jnp.where(causal, s, NEG_BIG)
        s_ref[n % n_slots, rows, :] = s
        part = jnp.max(s.reshape(tq // SUBLANES, SUBLANES, tq), axis=0)
        st["mx"] = part if st["mx"] is None else jnp.maximum(st["mx"], part)
        if kb == qi:
            st["m"] = jnp.max(st["mx"], axis=0, keepdims=True)

    def prob_tile(n, kb):
        rows = slice(kb * tq, (kb + 1) * tq)
        p = jnp.exp2(s_ref[n % n_slots, rows, :] - state[n]["m"])
        p_ref[n % n_slots, rows, :] = p.astype(jnp.bfloat16)

    def value_tile(n, kb):
        qi, hh, c = seqs[n]
        st = state[n]
        rows = slice(kb * tq, (kb + 1) * tq)
        part = jnp.dot(vaug_ref[hh, :, rows], p_ref[n % n_slots, rows, :],
                       preferred_element_type=jnp.float32)
        st["acc"] = part if kb == 0 else st["acc"] + part
        if kb < qi:
            return
        st["out"] = st["acc"][:V_DIM] * (1.0 / st["acc"][V_DIM:V_DIM + 1])
        if c == 1:
            o = state[n - 1]["out"] - lam * st["out"]
            ms = jnp.mean(o * o, axis=0, keepdims=True)
            o = o * lax.rsqrt(ms + EPS) * gsub_ref[...] * (1.0 - lam_init)
            feats = slice(hh * V_DIM, (hh + 1) * V_DIM)
            cols = slice(qi * tq, (qi + 1) * tq)
            z = zt_ref[0, feats, cols]
            gt_ref[0, feats, cols] = (o * (z * _sigmoid(z))).astype(jnp.bfloat16)

    stages = (score_tile, prob_tile, value_tile)
    for stage, n, kb in _attn_schedule([qi + 1 for qi, _, _ in seqs], n_slots, ATTN_LAG):
        stages[stage](n, kb)


def _attention(qt, k, vt, zt, lam_p, g_sub, *, n_heads, lam_init, tq=256):
    bsz, seq, _ = k.shape
    assert seq % tq == 0 and seq <= POS_SPLIT * POS_SPLIT
    hp = 2 if n_heads % 2 == 0 else 1
    slopes = jnp.exp2(-8.0 * jnp.arange(1, n_heads + 1, dtype=jnp.float32) / n_heads)
    feat = pl.BlockSpec((1, hp * V_DIM, seq), lambda b, h, s: (b, h, 0))
    grid_spec = pltpu.PrefetchScalarGridSpec(
        num_scalar_prefetch=1,
        grid=(bsz, n_heads // hp),
        in_specs=[
            feat,
            pl.BlockSpec((1, seq, hp * V_DIM), lambda b, h, s: (b, 0, h)),
            feat, feat,
            pl.BlockSpec((4, HEAD_DIM), lambda b, h, s: (0, 0)),
            pl.BlockSpec((V_DIM, tq), lambda b, h, s: (0, 0)),
        ],
        out_specs=feat,
        scratch_shapes=[
            pltpu.VMEM((hp, seq, 2 * V_DIM), jnp.bfloat16),
            pltpu.VMEM((hp, V_DIM + ONES_ROWS, seq), jnp.bfloat16),
            pltpu.VMEM((ATTN_SLOTS, seq, tq), jnp.float32),
            pltpu.VMEM((ATTN_SLOTS, seq, tq), jnp.bfloat16),
        ],
    )
    return pl.pallas_call(
        functools.partial(_attn_kernel, tq=tq, lam_init=lam_init),
        grid_spec=grid_spec,
        out_shape=jax.ShapeDtypeStruct((bsz, n_heads * V_DIM, seq), jnp.bfloat16),
        compiler_params=pltpu.CompilerParams(
            dimension_semantics=("arbitrary", "arbitrary"), vmem_limit_bytes=VMEM_LIMIT),
        name="diff_attn",
    )(slopes, qt, k, vt, zt, lam_p, jnp.broadcast_to(g_sub[:, None], (V_DIM, tq)))


def _attn_out_kernel(x_ref, gt_ref, w_ref, g_ref, y_ref):
    y = lax.dot_general(gt_ref[0], w_ref[...], (((0,), (0,)), ((), ())),
                        preferred_element_type=jnp.float32)
    y_ref[0] = x_ref[0] + _rms(y, g_ref[...])


def _attn_out(x, gt, w_out, g_post, *, tm=1024):
    bsz, seq, d = x.shape
    width = gt.shape[1]
    tile = pl.BlockSpec((1, tm, d), lambda b, t: (b, t, 0))
    return pl.pallas_call(
        _attn_out_kernel,
        grid=(bsz, seq // tm),
        in_specs=[tile, pl.BlockSpec((1, width, tm), lambda b, t: (b, 0, t)),
                  pl.BlockSpec((width, d), lambda b, t: (0, 0)),
                  pl.BlockSpec((1, d), lambda b, t: (0, 0))],
        out_specs=tile,
        out_shape=jax.ShapeDtypeStruct(x.shape, jnp.float32),
        compiler_params=pltpu.CompilerParams(
            dimension_semantics=("arbitrary", "arbitrary"), vmem_limit_bytes=VMEM_LIMIT),
        name="attn_out",
    )(x, gt, w_out.astype(jnp.bfloat16), g_post.reshape(1, -1))


def kernel(x, a_g_pre, a_w_in, a_w_dw, a_b_dw, a_ln_g, a_ln_b, a_w_out, a_g_post,
           kv_g, w_kv, b_g_pre, b_w_in, b_lambda, b_g_sub, b_w_out, b_g_post):
    n_a, n_b = a_w_in.shape[0], b_w_in.shape[0]
    width = b_w_out.shape[1]
    n_heads = width // V_DIM
    assert n_a >= 1 and w_kv.shape[1] == 2 * width and b_w_in.shape[2] == 2 * width

    qt = zt = k = vt = None
    for l in range(n_a):
        proj = None
        if l == n_a - 1 and n_b > 0:
            proj = (jnp.stack([b_g_pre[0], kv_g]), b_w_in[0][:, :width].T, b_w_in[0][:, width:].T,
                    w_kv[:, :width], w_kv[:, width:].T)
        x, *rest = _conv_block(x, a_g_pre[l], a_w_in[l], a_w_dw[l], a_b_dw[l], a_ln_g[l], a_ln_b[l],
                               a_w_out[l], a_g_post[l], proj)
        if rest:
            qt, zt, k, vt = rest

    for j in range(n_b):
        if j > 0:
            qt, zt = _proj(x, b_g_pre[j], b_w_in[j][:, :width].T, b_w_in[j][:, width:].T)
        lam_init = 0.8 - 0.6 * math.exp(-0.3 * (n_a + j + 1))
        gt = _attention(qt, k, vt, zt, b_lambda[j], b_g_sub[j], n_heads=n_heads, lam_init=lam_init)
        x = _attn_out(x, gt, b_w_out[j], b_g_post[j])
    return x
```

```python
import functools
import math

import jax
import jax.numpy as jnp
from jax import lax
from jax.experimental import pallas as pl
from jax.experimental.pallas import tpu as pltpu

EPS = 1e-6
HEAD_DIM = 64
V_DIM = 2 * HEAD_DIM
CONV_HALO = 32
CONV_ROWS = 64
LANES = 128
SUBLANES = 8
MXU_COLS = 256
POS_SPLIT = 256
POS_COLS = 6
ONES_ROWS = 16
LOG2E = 1.4426950408889634
Q_SCALE = HEAD_DIM ** -0.5 * LOG2E
ATTN_SLOTS = 4
ATTN_KEYS = 512
ATTN_LAG = 8
NEG_BIG = -1e30
VMEM_LIMIT = 56 * 1024 * 1024

_NT = (((1,), (1,)), ((), ()))


def _rms(x, g):
    return x * lax.rsqrt(jnp.mean(x * x, axis=-1, keepdims=True) + EPS) * g


def _sigmoid(x):
    return 0.5 * jnp.tanh(0.5 * x) + 0.5


def _interleave(xs, ys):
    out, i, j = [], 0, 0
    while i < len(xs) or j < len(ys):
        if j == len(ys) or (i < len(xs) and i * len(ys) <= j * len(xs)):
            out.append(xs[i])
            i += 1
        else:
            out.append(ys[j])
            j += 1
    return out


def _conv_block_kernel(*refs, ts, width, taps, with_proj):
    (x_ref, gpre_ref, win_ref, wdw_ref, bdw_ref, lng_ref, lnb_ref, wout_ref, gpost_ref) = refs[:9]
    refs = refs[9:]
    if with_proj:
        gains_ref, wq_ref, wz_ref, wk_ref, wv_ref = refs[:5]
        o_ref, qt_ref, zt_ref, k_ref, vt_ref = refs[5:10]
        refs = refs[10:]
    else:
        o_ref = refs[0]
        refs = refs[1:]
    cbuf_ref, zbuf_ref, ybuf_ref, gbuf_ref, y2buf_ref, hbuf_ref = refs
    t = pl.program_id(1)
    n_slab = width // LANES
    d = x_ref.shape[-1]
    first = CONV_HALO - (taps - 1)

    @pl.when(t == 0)
    def _():
        cbuf_ref[:, :, 0:CONV_HALO, :] = jnp.zeros((2, n_slab, CONV_HALO, LANES), jnp.float32)

    @pl.when(t > 0)
    def _():
        cbuf_ref[:, :, 0:CONV_HALO, :] = cbuf_ref[:, :, ts:ts + CONV_HALO, :]

    def in_glu(s, j):
        if j == 0:
            hbuf_ref[s, 0] = _rms(x_ref[s, 0], gpre_ref[...]).astype(jnp.bfloat16)
        h = hbuf_ref[s, 0]
        lo = j * MXU_COLS
        a = jnp.dot(h, win_ref[:, lo:lo + MXU_COLS], preferred_element_type=jnp.float32)
        b = jnp.dot(h, win_ref[:, width + lo:width + lo + MXU_COLS],
                    preferred_element_type=jnp.float32)
        c = a * _sigmoid(b)
        for q in range(MXU_COLS // LANES):
            cbuf_ref[s, j * (MXU_COLS // LANES) + q, CONV_HALO:CONV_HALO + ts, :] = (
                c[:, q * LANES:(q + 1) * LANES])

    def in_gate(s, j):
        lo = j * MXU_COLS
        z = jnp.dot(hbuf_ref[s, 0], win_ref[:, 2 * width + lo:2 * width + lo + MXU_COLS],
                    preferred_element_type=jnp.float32)
        zbuf_ref[s, :, lo:lo + MXU_COLS] = z * _sigmoid(z)

    def conv(s, j, i):
        r0 = i * CONV_ROWS
        acc = jnp.broadcast_to(bdw_ref[j], (CONV_ROWS, LANES))
        for k in range(taps):
            acc = acc + wdw_ref[j, k:k + 1, :] * cbuf_ref[s, j, r0 + first + k:r0 + first + k + CONV_ROWS, :]
        ybuf_ref[s, j, r0:r0 + CONV_ROWS, :] = acc

    def norm_gate(s, i):
        rows = slice(i * CONV_ROWS, (i + 1) * CONV_ROWS)
        y = jnp.concatenate([ybuf_ref[s, j, rows, :] for j in range(n_slab)], axis=1)
        mu = jnp.mean(y, axis=-1, keepdims=True)
        dev = y - mu
        var = jnp.mean(dev * dev, axis=-1, keepdims=True)
        y = dev * lax.rsqrt(var + EPS) * lng_ref[...] + lnb_ref[...]
        y = y * _sigmoid(y)
        gbuf_ref[s, rows, :] = (y * zbuf_ref[s, rows, :]).astype(jnp.bfloat16)

    def out_proj(s, n):
        lo = n * MXU_COLS
        y2buf_ref[s, :, lo:lo + MXU_COLS] = jnp.dot(
            gbuf_ref[s], wout_ref[:, lo:lo + MXU_COLS], preferred_element_type=jnp.float32)

    def residual(s):
        x1 = x_ref[s, 0] + _rms(y2buf_ref[s], gpost_ref[...])
        o_ref[s, 0] = x1
        if with_proj:
            xn = x1 * lax.rsqrt(jnp.mean(x1 * x1, axis=-1, keepdims=True) + EPS)
            hbuf_ref[s, 0] = (xn * gains_ref[0:1, :]).astype(jnp.bfloat16)
            hbuf_ref[s, 1] = (xn * gains_ref[1:2, :]).astype(jnp.bfloat16)

    def proj(s, which, n):
        cols = slice(n * MXU_COLS, (n + 1) * MXU_COLS)
        if which == 0:
            u = lax.dot_general(wq_ref[cols, :], hbuf_ref[s, 0], _NT, preferred_element_type=jnp.float32)
            qt_ref[s, 0, cols, :] = (u * Q_SCALE).astype(jnp.bfloat16)
        elif which == 1:
            zt_ref[s, 0, cols, :] = lax.dot_general(wz_ref[cols, :], hbuf_ref[s, 0], _NT,
                                                    preferred_element_type=jnp.float32)
        elif which == 2:
            u = jnp.dot(hbuf_ref[s, 1], wk_ref[:, cols], preferred_element_type=jnp.float32)
            k_ref[s, 0, :, cols] = u.astype(jnp.bfloat16)
        else:
            u = lax.dot_general(wv_ref[cols, :], hbuf_ref[s, 1], _NT, preferred_element_type=jnp.float32)
            vt_ref[s, 0, cols, :] = u.astype(jnp.bfloat16)

    task = functools.partial
    n_chunks = width // MXU_COLS

    def front(s):
        return ([task(in_glu, s, j) for j in range(n_chunks)]
                + [task(in_gate, s, j) for j in range(n_chunks)])

    def convs(s):
        return [task(conv, s, j, i) for j in range(n_slab) for i in range(ts // CONV_ROWS)]

    def back(s):
        tasks = ([task(norm_gate, s, i) for i in range(ts // CONV_ROWS)]
                 + [task(out_proj, s, n) for n in range(d // MXU_COLS)] + [task(residual, s)])
        if with_proj:
            tasks += [task(proj, s, which, n) for which in range(4) for n in range(n_chunks)]
        return tasks

    order = front(0) + _interleave(front(1), convs(0)) + _interleave(back(0), convs(1)) + back(1)
    for run in order:
        run()


def _conv_block(x, g_pre, w_in, w_dw, b_dw, ln_g, ln_b, w_out, g_post, proj=None, *, ts=256):
    bsz, seq, d = x.shape
    taps, width = w_dw.shape
    n_slab = width // LANES
    hb = bsz // 2
    assert bsz % 2 == 0 and taps - 1 <= CONV_HALO and seq % ts == 0 and ts % CONV_ROWS == 0
    assert width % MXU_COLS == 0 and d % MXU_COLS == 0
    row = lambda v: v.reshape(1, -1)
    bf16 = lambda w: w.astype(jnp.bfloat16)
    full = lambda a: pl.BlockSpec(a.shape, lambda b, t: (0,) * a.ndim, pipeline_mode=pl.Buffered(1))
    pair = lambda r, c: pl.BlockSpec((2, 1, r, c), lambda b, t: (0, b, t, 0))
    pair_t = lambda r, c: pl.BlockSpec((2, 1, r, c), lambda b, t: (0, b, 0, t))
    w_dw_slabs = w_dw.reshape(taps, n_slab, LANES).transpose(1, 0, 2)
    b_dw_slabs = b_dw.reshape(n_slab, 1, LANES)
    args = [x.reshape(2, hb, seq, d), row(g_pre), bf16(w_in), w_dw_slabs, b_dw_slabs, row(ln_g),
            row(ln_b), bf16(w_out), row(g_post)]
    in_specs = [pair(ts, d)] + [full(a) for a in args[1:]]
    out_specs = [pair(ts, d)]
    out_shapes = [jax.ShapeDtypeStruct((2, hb, seq, d), jnp.float32)]
    if proj is not None:
        gains, wq_t, wz_t, wk, wv_t = proj
        extra = [gains, bf16(wq_t), bf16(wz_t), bf16(wk), bf16(wv_t)]
        args += extra
        in_specs += [full(a) for a in extra]
        out_specs += [pair_t(width, ts), pair_t(width, ts), pair(ts, width), pair_t(width, ts)]
        out_shapes += [jax.ShapeDtypeStruct((2, hb, width, seq), jnp.bfloat16),
                       jax.ShapeDtypeStruct((2, hb, width, seq), jnp.float32),
                       jax.ShapeDtypeStruct((2, hb, seq, width), jnp.bfloat16),
                       jax.ShapeDtypeStruct((2, hb, width, seq), jnp.bfloat16)]
    outs = pl.pallas_call(
        functools.partial(_conv_block_kernel, ts=ts, width=width, taps=taps,
                          with_proj=proj is not None),
        grid=(hb, seq // ts),
        in_specs=in_specs, out_specs=out_specs, out_shape=out_shapes,
        scratch_shapes=[
            pltpu.VMEM((2, n_slab, CONV_HALO + ts, LANES), jnp.float32),
            pltpu.VMEM((2, ts, width), jnp.float32),
            pltpu.VMEM((2, n_slab, ts, LANES), jnp.float32),
            pltpu.VMEM((2, ts, width), jnp.bfloat16),
            pltpu.VMEM((2, ts, d), jnp.float32),
            pltpu.VMEM((2, 2, ts, d), jnp.bfloat16),
        ],
        compiler_params=pltpu.CompilerParams(
            dimension_semantics=("arbitrary", "arbitrary"), vmem_limit_bytes=VMEM_LIMIT),
        name="conv_block",
    )(*args)
    return [o.reshape(bsz, *o.shape[2:]) for o in outs]


def _proj_kernel(x_ref, gain_ref, wq_ref, wz_ref, qt_ref, zt_ref):
    h = _rms(x_ref[0], gain_ref[...]).astype(jnp.bfloat16)
    u = lax.dot_general(wq_ref[...], h, _NT, preferred_element_type=jnp.float32)
    qt_ref[0] = (u * Q_SCALE).astype(jnp.bfloat16)
    zt_ref[0] = lax.dot_general(wz_ref[...], h, _NT, preferred_element_type=jnp.float32)


def _proj(x, gain, wq_t, wz_t, *, tm=512):
    bsz, seq, d = x.shape
    width = wq_t.shape[0]
    feat = pl.BlockSpec((1, width, tm), lambda b, t: (b, 0, t))
    whole = lambda a: pl.BlockSpec(a.shape, lambda b, t: (0,) * a.ndim)
    args = [x, gain.reshape(1, -1), wq_t.astype(jnp.bfloat16), wz_t.astype(jnp.bfloat16)]
    return pl.pallas_call(
        _proj_kernel,
        grid=(bsz, seq // tm),
        in_specs=[pl.BlockSpec((1, tm, d), lambda b, t: (b, t, 0))] + [whole(a) for a in args[1:]],
        out_specs=[feat, feat],
        out_shape=[jax.ShapeDtypeStruct((bsz, width, seq), jnp.bfloat16),
                   jax.ShapeDtypeStruct((bsz, width, seq), jnp.float32)],
        compiler_params=pltpu.CompilerParams(
            dimension_semantics=("arbitrary", "arbitrary"), vmem_limit_bytes=VMEM_LIMIT),
        name="proj",
    )(*args)


def _attn_schedule(tiles_per_seq, slots, lag):
    flat = [(n, kb) for n, tiles in enumerate(tiles_per_seq) for kb in range(tiles)]
    done = [set(), set(), set()]
    ptr = [0, 0, 0]
    order = []

    def seq_done(stage, n):
        return n < 0 or all((n, kb) in done[stage] for kb in range(tiles_per_seq[n]))

    def ready(stage):
        if ptr[stage] == len(flat):
            return False
        n, kb = flat[ptr[stage]]
        if stage == 0:
            return seq_done(1, n - slots)
        if stage == 1:
            return seq_done(0, n) and seq_done(2, n - slots)
        return (n, kb) in done[1]

    while ptr[2] < len(flat):
        feeder_stuck = False
        for stage in range(3):
            ok = ready(stage)
            if ok and stage > 0:
                feeder_ended = ptr[stage - 1] == len(flat)
                ok = ptr[stage - 1] - ptr[stage] > lag or feeder_stuck or feeder_ended
            if ok:
                item = flat[ptr[stage]]
                order.append((stage, *item))
                done[stage].add(item)
                ptr[stage] += 1
            feeder_stuck = not ok and ptr[stage] < len(flat)
    return order


def _attn_kernel(slope_ref, qt_ref, k_ref, vt_ref, zt_ref, lam_ref, gsub_ref, gt_ref,
                 kaug_ref, vaug_ref, s_ref, p_ref, *, tq, lam_init):
    first = (pl.program_id(0) == 0) & (pl.program_id(1) == 0)
    n_heads = kaug_ref.shape[0]
    seq = k_ref.shape[1]
    n_slots = s_ref.shape[0]

    @pl.when(first)
    def _():
        j = lax.broadcasted_iota(jnp.int32, (seq, V_DIM), 0)
        lane = lax.broadcasted_iota(jnp.int32, (seq, V_DIM), 1)
        digit = jnp.where(lane % 2 == 0, j // POS_SPLIT, j % POS_SPLIT)
        digits = jnp.where(lane < POS_COLS, digit, 0).astype(jnp.float32).astype(jnp.bfloat16)
        for hh in range(n_heads):
            kaug_ref[hh, :, V_DIM:] = digits
            vaug_ref[hh, V_DIM:, :] = jnp.ones((ONES_ROWS, seq), jnp.bfloat16)

    for hh in range(n_heads):
        kaug_ref[hh, :, :V_DIM] = k_ref[0, :, hh * V_DIM:(hh + 1) * V_DIM]
        vaug_ref[hh, :V_DIM, :] = vt_ref[0, hh * V_DIM:(hh + 1) * V_DIM, :]

    r = lax.broadcasted_iota(jnp.int32, (ONES_ROWS, tq), 0)
    pos_rows = []
    for hh in range(n_heads):
        slope = slope_ref[pl.program_id(1) * n_heads + hh]
        c0 = jnp.full((ONES_ROWS, tq), slope * LOG2E, jnp.float32)
        c1 = c0.astype(jnp.bfloat16).astype(jnp.float32)
        c2 = (c0 - c1).astype(jnp.bfloat16).astype(jnp.float32)
        piece = jnp.where(r < 2, c1, jnp.where(r < 4, c2, c0 - c1 - c2))
        rows = jnp.where(r < POS_COLS, jnp.where(r % 2 == 0, piece * POS_SPLIT, piece), 0.0)
        pos_rows.append(rows.astype(jnp.bfloat16))
    zeros_half = jnp.zeros((HEAD_DIM, tq), jnp.bfloat16)
    zeros_tail = jnp.zeros((V_DIM - ONES_ROWS, tq), jnp.bfloat16)
    key_i = lax.broadcasted_iota(jnp.int32, (tq, tq), 0)
    qry_i = lax.broadcasted_iota(jnp.int32, (tq, tq), 1)
    causal = key_i <= qry_i

    lp = lam_ref[...]
    lam = (jnp.exp(jnp.sum(lp[0:1] * lp[1:2], axis=-1, keepdims=True))
           - jnp.exp(jnp.sum(lp[2:3] * lp[3:4], axis=-1, keepdims=True)) + lam_init)

    seqs = [(qi, hh, c) for qi in range(seq // tq) for hh in range(n_heads) for c in range(2)]
    key_tiles = [[slice(a, min(a + ATTN_KEYS, (qi + 1) * tq)) for a in range(0, (qi + 1) * tq, ATTN_KEYS)]
                 for qi, _, _ in seqs]
    state = {}

    def score_tile(n, kb):
        qi, hh, c = seqs[n]
        if kb == 0:
            qt = qt_ref[0, hh * V_DIM:(hh + 1) * V_DIM, qi * tq:(qi + 1) * tq]
            halves = (qt[:HEAD_DIM], zeros_half) if c == 0 else (zeros_half, qt[HEAD_DIM:])
            state[n] = dict(qa=jnp.concatenate([*halves, pos_rows[hh], zeros_tail], axis=0), mx=None)
        st = state[n]
        rows = key_tiles[n][kb]
        size = rows.stop - rows.start
        last = kb == len(key_tiles[n]) - 1
        s = jnp.dot(kaug_ref[hh, rows, :], st["qa"], preferred_element_type=jnp.float32)
        if last:
            diag = jnp.where(causal, s[size - tq:], NEG_BIG)
            s = diag if size == tq else jnp.concatenate([s[:size - tq], diag], axis=0)
        s_ref[n % n_slots, rows, :] = s
        part = jnp.max(s.reshape(size // SUBLANES, SUBLANES, tq), axis=0)
        st["mx"] = part if st["mx"] is None else jnp.maximum(st["mx"], part)
        if last:
            st["m"] = jnp.max(st["mx"], axis=0, keepdims=True)

    def prob_tile(n, kb):
        rows = key_tiles[n][kb]
        p = jnp.exp2(s_ref[n % n_slots, rows, :] - state[n]["m"])
        p_ref[n % n_slots, rows, :] = p.astype(jnp.bfloat16)

    def value_tile(n, kb):
        qi, hh, c = seqs[n]
        st = state[n]
        rows = key_tiles[n][kb]
        part = jnp.dot(vaug_ref[hh, :, rows], p_ref[n % n_slots, rows, :],
                       preferred_element_type=jnp.float32)
        st["acc"] = part if kb == 0 else st["acc"] + part
        if kb < len(key_tiles[n]) - 1:
            return
        st["out"] = st["acc"][:V_DIM] * (1.0 / st["acc"][V_DIM:V_DIM + 1])
        if c == 1:
            o = state[n - 1]["out"] - lam * st["out"]
            ms = jnp.mean(o * o, axis=0, keepdims=True)
            o = o * lax.rsqrt(ms + EPS) * gsub_ref[...] * (1.0 - lam_init)
            feats = slice(hh * V_DIM, (hh + 1) * V_DIM)
            cols = slice(qi * tq, (qi + 1) * tq)
            z = zt_ref[0, feats, cols]
            gt_ref[0, feats, cols] = (o * (z * _sigmoid(z))).astype(jnp.bfloat16)

    stages = (score_tile, prob_tile, value_tile)
    for stage, n, kb in _attn_schedule([len(tiles) for tiles in key_tiles], n_slots, ATTN_LAG):
        stages[stage](n, kb)


def _attention(qt, k, vt, zt, lam_p, g_sub, *, n_heads, lam_init, tq=256):
    bsz, seq, _ = k.shape
    assert seq % tq == 0 and seq <= POS_SPLIT * POS_SPLIT
    hp = 2 if n_heads % 2 == 0 else 1
    slopes = jnp.exp2(-8.0 * jnp.arange(1, n_heads + 1, dtype=jnp.float32) / n_heads)
    feat = pl.BlockSpec((1, hp * V_DIM, seq), lambda b, h, s: (b, h, 0))
    grid_spec = pltpu.PrefetchScalarGridSpec(
        num_scalar_prefetch=1,
        grid=(bsz, n_heads // hp),
        in_specs=[
            feat,
            pl.BlockSpec((1, seq, hp * V_DIM), lambda b, h, s: (b, 0, h)),
            feat, feat,
            pl.BlockSpec((4, HEAD_DIM), lambda b, h, s: (0, 0)),
            pl.BlockSpec((V_DIM, tq), lambda b, h, s: (0, 0)),
        ],
        out_specs=feat,
        scratch_shapes=[
            pltpu.VMEM((hp, seq, 2 * V_DIM), jnp.bfloat16),
            pltpu.VMEM((hp, V_DIM + ONES_ROWS, seq), jnp.bfloat16),
            pltpu.VMEM((ATTN_SLOTS, seq, tq), jnp.float32),
            pltpu.VMEM((ATTN_SLOTS, seq, tq), jnp.bfloat16),
        ],
    )
    return pl.pallas_call(
        functools.partial(_attn_kernel, tq=tq, lam_init=lam_init),
        grid_spec=grid_spec,
        out_shape=jax.ShapeDtypeStruct((bsz, n_heads * V_DIM, seq), jnp.bfloat16),
        compiler_params=pltpu.CompilerParams(
            dimension_semantics=("arbitrary", "arbitrary"), vmem_limit_bytes=VMEM_LIMIT),
        name="diff_attn",
    )(slopes, qt, k, vt, zt, lam_p, jnp.broadcast_to(g_sub[:, None], (V_DIM, tq)))


def _attn_out_kernel(x_ref, gt_ref, w_ref, g_ref, y_ref):
    y = lax.dot_general(gt_ref[0], w_ref[...], (((0,), (0,)), ((), ())),
                        preferred_element_type=jnp.float32)
    y_ref[0] = x_ref[0] + _rms(y, g_ref[...])


def _attn_out(x, gt, w_out, g_post, *, tm=1024):
    bsz, seq, d = x.shape
    width = gt.shape[1]
    tile = pl.BlockSpec((1, tm, d), lambda b, t: (b, t, 0))
    return pl.pallas_call(
        _attn_out_kernel,
        grid=(bsz, seq // tm),
        in_specs=[tile, pl.BlockSpec((1, width, tm), lambda b, t: (b, 0, t)),
                  pl.BlockSpec((width, d), lambda b, t: (0, 0)),
                  pl.BlockSpec((1, d), lambda b, t: (0, 0))],
        out_specs=tile,
        out_shape=jax.ShapeDtypeStruct(x.shape, jnp.float32),
        compiler_params=pltpu.CompilerParams(
            dimension_semantics=("arbitrary", "arbitrary"), vmem_limit_bytes=VMEM_LIMIT),
        name="attn_out",
    )(x, gt, w_out.astype(jnp.bfloat16), g_post.reshape(1, -1))


def kernel(x, a_g_pre, a_w_in, a_w_dw, a_b_dw, a_ln_g, a_ln_b, a_w_out, a_g_post,
           kv_g, w_kv, b_g_pre, b_w_in, b_lambda, b_g_sub, b_w_out, b_g_post):
    n_a, n_b = a_w_in.shape[0], b_w_in.shape[0]
    width = b_w_out.shape[1]
    n_heads = width // V_DIM
    assert n_a >= 1 and w_kv.shape[1] == 2 * width and b_w_in.shape[2] == 2 * width

    qt = zt = k = vt = None
    for l in range(n_a):
        proj = None
        if l == n_a - 1 and n_b > 0:
            proj = (jnp.stack([b_g_pre[0], kv_g]), b_w_in[0][:, :width].T, b_w_in[0][:, width:].T,
                    w_kv[:, :width], w_kv[:, width:].T)
        x, *rest = _conv_block(x, a_g_pre[l], a_w_in[l], a_w_dw[l], a_b_dw[l], a_ln_g[l], a_ln_b[l],
                               a_w_out[l], a_g_post[l], proj)
        if rest:
            qt, zt, k, vt = rest

    for j in range(n_b):
        if j > 0:
            qt, zt = _proj(x, b_g_pre[j], b_w_in[j][:, :width].T, b_w_in[j][:, width:].T)
        lam_init = 0.8 - 0.6 * math.exp(-0.3 * (n_a + j + 1))
        gt = _attention(qt, k, vt, zt, b_lambda[j], b_g_sub[j], n_heads=n_heads, lam_init=lam_init)
        x = _attn_out(x, gt, b_w_out[j], b_g_post[j])
    return x
```

```python
import functools
import math

import jax
import jax.numpy as jnp
from jax import lax
from jax.experimental import pallas as pl
from jax.experimental.pallas import tpu as pltpu

EPS = 1e-6
HEAD_DIM = 64
V_DIM = 2 * HEAD_DIM
CONV_HALO = 32
CONV_ROWS = 64
CONV_GROUP = 4
PACK_ROWS = 16
LANES = 128
SUBLANES = 8
MXU_COLS = 256
POS_SPLIT = 256
POS_COLS = 6
ONES_ROWS = 16
LOG2E = 1.4426950408889634
Q_SCALE = HEAD_DIM ** -0.5 * LOG2E
ATTN_SLOTS = 4
ATTN_KEYS = 512
ATTN_LAG = 8
NEG_BIG = -1e30
VMEM_LIMIT = 56 * 1024 * 1024

_NT = (((1,), (1,)), ((), ()))


def _rms(x, g):
    return x * lax.rsqrt(jnp.mean(x * x, axis=-1, keepdims=True) + EPS) * g


def _sigmoid(x):
    return 0.5 * jnp.tanh(0.5 * x) + 0.5


def _interleave(xs, ys):
    out, i, j = [], 0, 0
    while i < len(xs) or j < len(ys):
        if j == len(ys) or (i < len(xs) and i * len(ys) <= j * len(xs)):
            out.append(xs[i])
            i += 1
        else:
            out.append(ys[j])
            j += 1
    return out


def _conv_block_kernel(*refs, ts, width, taps, with_proj):
    (x_ref, gpre_ref, win_ref, wdw_ref, bdw_ref, lng_ref, lnb_ref, wout_ref, gpost_ref) = refs[:9]
    refs = refs[9:]
    if with_proj:
        gains_ref, wq_ref, wz_ref, wk_ref, wv_ref = refs[:5]
        o_ref, qt_ref, zt_ref, k_ref, vt_ref = refs[5:10]
        refs = refs[10:]
    else:
        o_ref = refs[0]
        refs = refs[1:]
    cbuf_ref, cpk_ref, zbuf_ref, ybuf_ref, gbuf_ref, y2buf_ref, hbuf_ref = refs
    t = pl.program_id(1)
    n_slab = width // LANES
    d = x_ref.shape[-1]
    first = CONV_HALO - (taps - 1)
    n_rows = CONV_HALO + ts

    @pl.when(t == 0)
    def _():
        cbuf_ref[:, :, 0:CONV_HALO, :] = jnp.zeros((2, n_slab, CONV_HALO, LANES), jnp.float32)
        cbuf_ref[:, :, n_rows:, :] = jnp.zeros((2, n_slab, PACK_ROWS, LANES), jnp.float32)

    @pl.when(t > 0)
    def _():
        cbuf_ref[:, :, 0:CONV_HALO, :] = cbuf_ref[:, :, ts:ts + CONV_HALO, :]

    def in_glu(s, j):
        if j == 0:
            hbuf_ref[s, 0] = _rms(x_ref[s, 0], gpre_ref[...]).astype(jnp.bfloat16)
        h = hbuf_ref[s, 0]
        lo = j * MXU_COLS
        a = jnp.dot(h, win_ref[:, lo:lo + MXU_COLS], preferred_element_type=jnp.float32)
        b = jnp.dot(h, win_ref[:, width + lo:width + lo + MXU_COLS],
                    preferred_element_type=jnp.float32)
        c = a * _sigmoid(b)
        for q in range(MXU_COLS // LANES):
            cbuf_ref[s, j * (MXU_COLS // LANES) + q, CONV_HALO:CONV_HALO + ts, :] = (
                c[:, q * LANES:(q + 1) * LANES])

    def in_gate(s, j):
        lo = j * MXU_COLS
        z = jnp.dot(hbuf_ref[s, 0], win_ref[:, 2 * width + lo:2 * width + lo + MXU_COLS],
                    preferred_element_type=jnp.float32)
        zbuf_ref[s, :, lo:lo + MXU_COLS] = z * _sigmoid(z)

    def pack(s, j):
        for par in range(2):
            rows = cbuf_ref[s, j, par:par + n_rows, :].astype(jnp.bfloat16)
            cpk_ref[s, par, j] = pltpu.bitcast(rows, jnp.uint32)

    def conv(s, j, i):
        r0 = i * CONV_ROWS
        n_groups = CONV_ROWS // PACK_ROWS
        bias = jnp.broadcast_to(bdw_ref[j], (SUBLANES, LANES))
        acc = [[bias, bias] for _ in range(n_groups)]
        for k0 in range(0, taps, CONV_GROUP):
            ks = range(k0, min(k0 + CONV_GROUP, taps))
            ws = [jnp.broadcast_to(wdw_ref[j, k:k + 1, :], (PACK_ROWS, LANES)).astype(jnp.bfloat16)
                  for k in ks]
            for g in range(n_groups):
                total = None
                for w, k in zip(ws, ks):
                    row = r0 + PACK_ROWS * g + first + k
                    word = row // 2
                    win = pltpu.bitcast(cpk_ref[s, row % 2, j, word:word + SUBLANES, :], jnp.bfloat16)
                    total = w * win if total is None else total + w * win
                words = pltpu.bitcast(total, jnp.uint32)
                for par in range(2):
                    acc[g][par] = acc[g][par] + pltpu.unpack_elementwise(
                        words, index=par, packed_dtype=jnp.bfloat16, unpacked_dtype=jnp.float32)
        for g in range(n_groups):
            for par in range(2):
                ybuf_ref[s, j, pl.ds(r0 + PACK_ROWS * g + par, SUBLANES, stride=2), :] = acc[g][par]

    def norm_gate(s, i):
        rows = slice(i * CONV_ROWS, (i + 1) * CONV_ROWS)
        y = jnp.concatenate([ybuf_ref[s, j, rows, :] for j in range(n_slab)], axis=1)
        mu = jnp.mean(y, axis=-1, keepdims=True)
        dev = y - mu
        var = jnp.mean(dev * dev, axis=-1, keepdims=True)
        y = dev * lax.rsqrt(var + EPS) * lng_ref[...] + lnb_ref[...]
        y = y * _sigmoid(y)
        gbuf_ref[s, rows, :] = (y * zbuf_ref[s, rows, :]).astype(jnp.bfloat16)

    def out_proj(s, n):
        lo = n * MXU_COLS
        y2buf_ref[s, :, lo:lo + MXU_COLS] = jnp.dot(
            gbuf_ref[s], wout_ref[:, lo:lo + MXU_COLS], preferred_element_type=jnp.float32)

    def residual(s):
        x1 = x_ref[s, 0] + _rms(y2buf_ref[s], gpost_ref[...])
        o_ref[s, 0] = x1
        if with_proj:
            xn = x1 * lax.rsqrt(jnp.mean(x1 * x1, axis=-1, keepdims=True) + EPS)
            hbuf_ref[s, 0] = (xn * gains_ref[0:1, :]).astype(jnp.bfloat16)
            hbuf_ref[s, 1] = (xn * gains_ref[1:2, :]).astype(jnp.bfloat16)

    def proj(s, which, n):
        cols = slice(n * MXU_COLS, (n + 1) * MXU_COLS)
        if which == 0:
            u = lax.dot_general(wq_ref[cols, :], hbuf_ref[s, 0], _NT, preferred_element_type=jnp.float32)
            qt_ref[s, 0, cols, :] = (u * Q_SCALE).astype(jnp.bfloat16)
        elif which == 1:
            zt_ref[s, 0, cols, :] = lax.dot_general(wz_ref[cols, :], hbuf_ref[s, 0], _NT,
                                                    preferred_element_type=jnp.float32)
        elif which == 2:
            u = jnp.dot(hbuf_ref[s, 1], wk_ref[:, cols], preferred_element_type=jnp.float32)
            k_ref[s, 0, :, cols] = u.astype(jnp.bfloat16)
        else:
            u = lax.dot_general(wv_ref[cols, :], hbuf_ref[s, 1], _NT, preferred_element_type=jnp.float32)
            vt_ref[s, 0, cols, :] = u.astype(jnp.bfloat16)

    task = functools.partial
    n_chunks = width // MXU_COLS

    def front(s):
        return ([task(in_glu, s, j) for j in range(n_chunks)]
                + [task(in_gate, s, j) for j in range(n_chunks)])

    def convs(s):
        return ([task(pack, s, j) for j in range(n_slab)]
                + [task(conv, s, j, i) for j in range(n_slab) for i in range(ts // CONV_ROWS)])

    def back(s):
        tasks = ([task(norm_gate, s, i) for i in range(ts // CONV_ROWS)]
                 + [task(out_proj, s, n) for n in range(d // MXU_COLS)] + [task(residual, s)])
        if with_proj:
            tasks += [task(proj, s, which, n) for which in range(4) for n in range(n_chunks)]
        return tasks

    order = front(0) + _interleave(front(1), convs(0)) + _interleave(back(0), convs(1)) + back(1)
    for run in order:
        run()


def _conv_block(x, g_pre, w_in, w_dw, b_dw, ln_g, ln_b, w_out, g_post, proj=None, *, ts=256):
    bsz, seq, d = x.shape
    taps, width = w_dw.shape
    n_slab = width // LANES
    hb = bsz // 2
    assert bsz % 2 == 0 and taps - 1 <= CONV_HALO and seq % ts == 0 and ts % CONV_ROWS == 0
    assert width % MXU_COLS == 0 and d % MXU_COLS == 0
    assert CONV_ROWS % PACK_ROWS == 0 and (CONV_HALO + ts) % PACK_ROWS == 0
    row = lambda v: v.reshape(1, -1)
    bf16 = lambda w: w.astype(jnp.bfloat16)
    full = lambda a: pl.BlockSpec(a.shape, lambda b, t: (0,) * a.ndim, pipeline_mode=pl.Buffered(1))
    pair = lambda r, c: pl.BlockSpec((2, 1, r, c), lambda b, t: (0, b, t, 0))
    pair_t = lambda r, c: pl.BlockSpec((2, 1, r, c), lambda b, t: (0, b, 0, t))
    w_dw_slabs = w_dw.reshape(taps, n_slab, LANES).transpose(1, 0, 2)
    b_dw_slabs = b_dw.reshape(n_slab, 1, LANES)
    args = [x.reshape(2, hb, seq, d), row(g_pre), bf16(w_in), w_dw_slabs, b_dw_slabs, row(ln_g),
            row(ln_b), bf16(w_out), row(g_post)]
    in_specs = [pair(ts, d)] + [full(a) for a in args[1:]]
    out_specs = [pair(ts, d)]
    out_shapes = [jax.ShapeDtypeStruct((2, hb, seq, d), jnp.float32)]
    if proj is not None:
        gains, wq_t, wz_t, wk, wv_t = proj
        extra = [gains, bf16(wq_t), bf16(wz_t), bf16(wk), bf16(wv_t)]
        args += extra
        in_specs += [full(a) for a in extra]
        out_specs += [pair_t(width, ts), pair_t(width, ts), pair(ts, width), pair_t(width, ts)]
        out_shapes += [jax.ShapeDtypeStruct((2, hb, width, seq), jnp.bfloat16),
                       jax.ShapeDtypeStruct((2, hb, width, seq), jnp.float32),
                       jax.ShapeDtypeStruct((2, hb, seq, width), jnp.bfloat16),
                       jax.ShapeDtypeStruct((2, hb, width, seq), jnp.bfloat16)]
    outs = pl.pallas_call(
        functools.partial(_conv_block_kernel, ts=ts, width=width, taps=taps,
                          with_proj=proj is not None),
        grid=(hb, seq // ts),
        in_specs=in_specs, out_specs=out_specs, out_shape=out_shapes,
        scratch_shapes=[
            pltpu.VMEM((2, n_slab, CONV_HALO + ts + PACK_ROWS, LANES), jnp.float32),
            pltpu.VMEM((2, 2, n_slab, (CONV_HALO + ts) // 2, LANES), jnp.uint32),
            pltpu.VMEM((2, ts, width), jnp.float32),
            pltpu.VMEM((2, n_slab, ts, LANES), jnp.float32),
            pltpu.VMEM((2, ts, width), jnp.bfloat16),
            pltpu.VMEM((2, ts, d), jnp.float32),
            pltpu.VMEM((2, 2, ts, d), jnp.bfloat16),
        ],
        compiler_params=pltpu.CompilerParams(
            dimension_semantics=("arbitrary", "arbitrary"), vmem_limit_bytes=VMEM_LIMIT),
        name="conv_block",
    )(*args)
    return [o.reshape(bsz, *o.shape[2:]) for o in outs]


def _proj_kernel(x_ref, gain_ref, wq_ref, wz_ref, qt_ref, zt_ref):
    h = _rms(x_ref[0], gain_ref[...]).astype(jnp.bfloat16)
    u = lax.dot_general(wq_ref[...], h, _NT, preferred_element_type=jnp.float32)
    qt_ref[0] = (u * Q_SCALE).astype(jnp.bfloat16)
    zt_ref[0] = lax.dot_general(wz_ref[...], h, _NT, preferred_element_type=jnp.float32)


def _proj(x, gain, wq_t, wz_t, *, tm=512):
    bsz, seq, d = x.shape
    width = wq_t.shape[0]
    feat = pl.BlockSpec((1, width, tm), lambda b, t: (b, 0, t))
    whole = lambda a: pl.BlockSpec(a.shape, lambda b, t: (0,) * a.ndim)
    args = [x, gain.reshape(1, -1), wq_t.astype(jnp.bfloat16), wz_t.astype(jnp.bfloat16)]
    return pl.pallas_call(
        _proj_kernel,
        grid=(bsz, seq // tm),
        in_specs=[pl.BlockSpec((1, tm, d), lambda b, t: (b, t, 0))] + [whole(a) for a in args[1:]],
        out_specs=[feat, feat],
        out_shape=[jax.ShapeDtypeStruct((bsz, width, seq), jnp.bfloat16),
                   jax.ShapeDtypeStruct((bsz, width, seq), jnp.float32)],
        compiler_params=pltpu.CompilerParams(
            dimension_semantics=("arbitrary", "arbitrary"), vmem_limit_bytes=VMEM_LIMIT),
        name="proj",
    )(*args)


def _attn_schedule(tiles_per_seq, slots, lag):
    flat = [(n, kb) for n, tiles in enumerate(tiles_per_seq) for kb in range(tiles)]
    done = [set(), set(), set()]
    ptr = [0, 0, 0]
    order = []

    def seq_done(stage, n):
        return n < 0 or all((n, kb) in done[stage] for kb in range(tiles_per_seq[n]))

    def ready(stage):
        if ptr[stage] == len(flat):
            return False
        n, kb = flat[ptr[stage]]
        if stage == 0:
            return seq_done(1, n - slots)
        if stage == 1:
            return seq_done(0, n) and seq_done(2, n - slots)
        return (n, kb) in done[1]

    while ptr[2] < len(flat):
        feeder_stuck = False
        for stage in range(3):
            ok = ready(stage)
            if ok and stage > 0:
                feeder_ended = ptr[stage - 1] == len(flat)
                ok = ptr[stage - 1] - ptr[stage] > lag or feeder_stuck or feeder_ended
            if ok:
                item = flat[ptr[stage]]
                order.append((stage, *item))
                done[stage].add(item)
                ptr[stage] += 1
            feeder_stuck = not ok and ptr[stage] < len(flat)
    return order


def _attn_kernel(slope_ref, qt_ref, k_ref, vt_ref, zt_ref, lam_ref, gsub_ref, gt_ref,
                 kaug_ref, vaug_ref, s_ref, p_ref, *, tq, lam_init):
    first = (pl.program_id(0) == 0) & (pl.program_id(1) == 0)
    n_heads = kaug_ref.shape[0]
    seq = k_ref.shape[1]
    n_slots = s_ref.shape[0]

    @pl.when(first)
    def _():
        j = lax.broadcasted_iota(jnp.int32, (seq, V_DIM), 0)
        lane = lax.broadcasted_iota(jnp.int32, (seq, V_DIM), 1)
        digit = jnp.where(lane % 2 == 0, j // POS_SPLIT, j % POS_SPLIT)
        digits = jnp.where(lane < POS_COLS, digit, 0).astype(jnp.float32).astype(jnp.bfloat16)
        for hh in range(n_heads):
            kaug_ref[hh, :, V_DIM:] = digits
            vaug_ref[hh, V_DIM:, :] = jnp.ones((ONES_ROWS, seq), jnp.bfloat16)

    for hh in range(n_heads):
        kaug_ref[hh, :, :V_DIM] = k_ref[0, :, hh * V_DIM:(hh + 1) * V_DIM]
        vaug_ref[hh, :V_DIM, :] = vt_ref[0, hh * V_DIM:(hh + 1) * V_DIM, :]

    r = lax.broadcasted_iota(jnp.int32, (ONES_ROWS, tq), 0)
    pos_rows = []
    for hh in range(n_heads):
        slope = slope_ref[pl.program_id(1) * n_heads + hh]
        c0 = jnp.full((ONES_ROWS, tq), slope * LOG2E, jnp.float32)
        c1 = c0.astype(jnp.bfloat16).astype(jnp.float32)
        c2 = (c0 - c1).astype(jnp.bfloat16).astype(jnp.float32)
        piece = jnp.where(r < 2, c1, jnp.where(r < 4, c2, c0 - c1 - c2))
        rows = jnp.where(r < POS_COLS, jnp.where(r % 2 == 0, piece * POS_SPLIT, piece), 0.0)
        pos_rows.append(rows.astype(jnp.bfloat16))
    zeros_half = jnp.zeros((HEAD_DIM, tq), jnp.bfloat16)
    zeros_tail = jnp.zeros((V_DIM - ONES_ROWS, tq), jnp.bfloat16)
    key_i = lax.broadcasted_iota(jnp.int32, (tq, tq), 0)
    qry_i = lax.broadcasted_iota(jnp.int32, (tq, tq), 1)
    causal = key_i <= qry_i

    lp = lam_ref[...]
    lam = (jnp.exp(jnp.sum(lp[0:1] * lp[1:2], axis=-1, keepdims=True))
           - jnp.exp(jnp.sum(lp[2:3] * lp[3:4], axis=-1, keepdims=True)) + lam_init)

    seqs = [(qi, hh, c) for qi in range(seq // tq) for hh in range(n_heads) for c in range(2)]
    key_tiles = [[slice(a, min(a + ATTN_KEYS, (qi + 1) * tq)) for a in range(0, (qi + 1) * tq, ATTN_KEYS)]
                 for qi, _, _ in seqs]
    state = {}

    def score_tile(n, kb):
        qi, hh, c = seqs[n]
        if kb == 0:
            qt = qt_ref[0, hh * V_DIM:(hh + 1) * V_DIM, qi * tq:(qi + 1) * tq]
            halves = (qt[:HEAD_DIM], zeros_half) if c == 0 else (zeros_half, qt[HEAD_DIM:])
            state[n] = dict(qa=jnp.concatenate([*halves, pos_rows[hh], zeros_tail], axis=0), mx=None)
        st = state[n]
        rows = key_tiles[n][kb]
        size = rows.stop - rows.start
        last = kb == len(key_tiles[n]) - 1
        s = jnp.dot(kaug_ref[hh, rows, :], st["qa"], preferred_element_type=jnp.float32)
        if last:
            diag = jnp.where(causal, s[size - tq:], NEG_BIG)
            s = diag if size == tq else jnp.concatenate([s[:size - tq], diag], axis=0)
        s_ref[n % n_slots, rows, :] = s
        part = jnp.max(s.reshape(size // SUBLANES, SUBLANES, tq), axis=0)
        st["mx"] = part if st["mx"] is None else jnp.maximum(st["mx"], part)
        if last:
            st["m"] = jnp.max(st["mx"], axis=0, keepdims=True)

    def prob_tile(n, kb):
        rows = key_tiles[n][kb]
        p = jnp.exp2(s_ref[n % n_slots, rows, :] - state[n]["m"])
        p_ref[n % n_slots, rows, :] = p.astype(jnp.bfloat16)

    def value_tile(n, kb):
        qi, hh, c = seqs[n]
        st = state[n]
        rows = key_tiles[n][kb]
        part = jnp.dot(vaug_ref[hh, :, rows], p_ref[n % n_slots, rows, :],
                       preferred_element_type=jnp.float32)
        st["acc"] = part if kb == 0 else st["acc"] + part
        if kb < len(key_tiles[n]) - 1:
            return
        st["out"] = st["acc"][:V_DIM] * (1.0 / st["acc"][V_DIM:V_DIM + 1])
        if c == 1:
            o = state[n - 1]["out"] - lam * st["out"]
            ms = jnp.mean(o * o, axis=0, keepdims=True)
            o = o * lax.rsqrt(ms + EPS) * gsub_ref[...] * (1.0 - lam_init)
            feats = slice(hh * V_DIM, (hh + 1) * V_DIM)
            cols = slice(qi * tq, (qi + 1) * tq)
            z = zt_ref[0, feats, cols]
            gt_ref[0, feats, cols] = (o * (z * _sigmoid(z))).astype(jnp.bfloat16)

    stages = (score_tile, prob_tile, value_tile)
    for stage, n, kb in _attn_schedule([len(tiles) for tiles in key_tiles], n_slots, ATTN_LAG):
        stages[stage](n, kb)


def _attention(qt, k, vt, zt, lam_p, g_sub, *, n_heads, lam_init, tq=256):
    bsz, seq, _ = k.shape
    assert seq % tq == 0 and seq <= POS_SPLIT * POS_SPLIT
    hp = 2 if n_heads % 2 == 0 else 1
    slopes = jnp.exp2(-8.0 * jnp.arange(1, n_heads + 1, dtype=jnp.float32) / n_heads)
    feat = pl.BlockSpec((1, hp * V_DIM, seq), lambda b, h, s: (b, h, 0))
    grid_spec = pltpu.PrefetchScalarGridSpec(
        num_scalar_prefetch=1,
        grid=(bsz, n_heads // hp),
        in_specs=[
            feat,
            pl.BlockSpec((1, seq, hp * V_DIM), lambda b, h, s: (b, 0, h)),
            feat, feat,
            pl.BlockSpec((4, HEAD_DIM), lambda b, h, s: (0, 0)),
            pl.BlockSpec((V_DIM, tq), lambda b, h, s: (0, 0)),
        ],
        out_specs=feat,
        scratch_shapes=[
            pltpu.VMEM((hp, seq, 2 * V_DIM), jnp.bfloat16),
            pltpu.VMEM((hp, V_DIM + ONES_ROWS, seq), jnp.bfloat16),
            pltpu.VMEM((ATTN_SLOTS, seq, tq), jnp.float32),
            pltpu.VMEM((ATTN_SLOTS, seq, tq), jnp.bfloat16),
        ],
    )
    return pl.pallas_call(
        functools.partial(_attn_kernel, tq=tq, lam_init=lam_init),
        grid_spec=grid_spec,
        out_shape=jax.ShapeDtypeStruct((bsz, n_heads * V_DIM, seq), jnp.bfloat16),
        compiler_params=pltpu.CompilerParams(
            dimension_semantics=("arbitrary", "arbitrary"), vmem_limit_bytes=VMEM_LIMIT),
        name="diff_attn",
    )(slopes, qt, k, vt, zt, lam_p, jnp.broadcast_to(g_sub[:, None], (V_DIM, tq)))


def _attn_out_kernel(x_ref, gt_ref, w_ref, g_ref, y_ref):
    y = lax.dot_general(gt_ref[0], w_ref[...], (((0,), (0,)), ((), ())),
                        preferred_element_type=jnp.float32)
    y_ref[0] = x_ref[0] + _rms(y, g_ref[...])


def _attn_out(x, gt, w_out, g_post, *, tm=1024):
    bsz, seq, d = x.shape
    width = gt.shape[1]
    tile = pl.BlockSpec((1, tm, d), lambda b, t: (b, t, 0))
    return pl.pallas_call(
        _attn_out_kernel,
        grid=(bsz, seq // tm),
        in_specs=[tile, pl.BlockSpec((1, width, tm), lambda b, t: (b, 0, t)),
                  pl.BlockSpec((width, d), lambda b, t: (0, 0)),
                  pl.BlockSpec((1, d), lambda b, t: (0, 0))],
        out_specs=tile,
        out_shape=jax.ShapeDtypeStruct(x.shape, jnp.float32),
        compiler_params=pltpu.CompilerParams(
            dimension_semantics=("arbitrary", "arbitrary"), vmem_limit_bytes=VMEM_LIMIT),
        name="attn_out",
    )(x, gt, w_out.astype(jnp.bfloat16), g_post.reshape(1, -1))


def kernel(x, a_g_pre, a_w_in, a_w_dw, a_b_dw, a_ln_g, a_ln_b, a_w_out, a_g_post,
           kv_g, w_kv, b_g_pre, b_w_in, b_lambda, b_g_sub, b_w_out, b_g_post):
    n_a, n_b = a_w_in.shape[0], b_w_in.shape[0]
    width = b_w_out.shape[1]
    n_heads = width // V_DIM
    assert n_a >= 1 and w_kv.shape[1] == 2 * width and b_w_in.shape[2] == 2 * width

    qt = zt = k = vt = None
    for l in range(n_a):
        proj = None
        if l == n_a - 1 and n_b > 0:
            proj = (jnp.stack([b_g_pre[0], kv_g]), b_w_in[0][:, :width].T, b_w_in[0][:, width:].T,
                    w_kv[:, :width], w_kv[:, width:].T)
        x, *rest = _conv_block(x, a_g_pre[l], a_w_in[l], a_w_dw[l], a_b_dw[l], a_ln_g[l], a_ln_b[l],
                               a_w_out[l], a_g_post[l], proj)
        if rest:
            qt, zt, k, vt = rest

    for j in range(n_b):
        if j > 0:
            qt, zt = _proj(x, b_g_pre[j], b_w_in[j][:, :width].T, b_w_in[j][:, width:].T)
        lam_init = 0.8 - 0.6 * math.exp(-0.3 * (n_a + j + 1))
        gt = _attention(qt, k, vt, zt, b_lambda[j], b_g_sub[j], n_heads=n_heads, lam_init=lam_init)
        x = _attn_out(x, gt, b_w_out[j], b_g_post[j])
    return x
```
